```python
import math
import jax, jax.numpy as jnp
from jax import lax
import numpy as np

D_MODEL = 2048
BATCH = 8
SEQ = 2048
DEPTH = 1

HEAD_DIM = 128
GQA_Q_HEADS = 8
GQA_KV_HEADS = 2
NA_HEADS = 4
MEM_HEADS = 4
MEM_LEN = 256
D_FF = 5632
GRID_W = 64
NA_ROWS_MAX = 8
NA_COLS = 16
Q_BLOCK = 128
ROPE_THETA = 10000.0
AXIS_ROT_DIM = HEAD_DIM // 2
LN_EPS = 1e-5
RMS_EPS = 1e-6
N_BRANCHES = 3
DEEPNORM_ALPHA = (2 * DEPTH) ** 0.25
DEEPNORM_BETA = (8 * DEPTH) ** -0.25

WA_Q = GQA_Q_HEADS * HEAD_DIM
WA_KV = GQA_KV_HEADS * HEAD_DIM
WB = NA_HEADS * HEAD_DIM
WM = MEM_HEADS * HEAD_DIM
SPLITS = [WA_Q, WA_Q + WA_KV, WA_Q + 2 * WA_KV, WA_Q + 2 * WA_KV + 3 * WB,
          WA_Q + 2 * WA_KV + 3 * WB + WM]
W_IN_COLS = WA_Q + 2 * WA_KV + 3 * WB + WM + N_BRANCHES * D_MODEL

kernel_name = "hybrid_gqa_natten_memory_macaron_deepnorm"


def layer_norm(x, g, b):
    xf = x.astype(jnp.float32)
    mu = xf.mean(-1, keepdims=True)
    var = jnp.square(xf - mu).mean(-1, keepdims=True)
    y = (xf - mu) * lax.rsqrt(var + LN_EPS) * g.astype(jnp.float32) + b.astype(jnp.float32)
    return y.astype(x.dtype)


def head_rms_norm(x, g):
    xf = x.astype(jnp.float32)
    y = xf * lax.rsqrt(jnp.mean(xf * xf, axis=-1, keepdims=True) + RMS_EPS) * g.astype(jnp.float32)
    return y.astype(x.dtype)


def swiglu(x, w_gu, w_down):
    gate, up = jnp.split(x @ w_gu, 2, axis=-1)
    return (jax.nn.silu(gate) * up) @ w_down


def axial_tables(seq):
    t = jnp.arange(seq)
    row = (t // GRID_W).astype(jnp.float32)
    col = (t % GRID_W).astype(jnp.float32)
    inv = ROPE_THETA ** (-jnp.arange(0, AXIS_ROT_DIM, 2, dtype=jnp.float32) / AXIS_ROT_DIM)
    ang_r = row[:, None] * inv[None, :]
    ang_c = col[:, None] * inv[None, :]
    return jnp.cos(ang_r), jnp.sin(ang_r), jnp.cos(ang_c), jnp.sin(ang_c)


def axial_rope(x, cos_r, sin_r, cos_c, sin_c):
    xf = x.astype(jnp.float32)

    def rot(u, c, s):
        u1, u2 = jnp.split(u, 2, axis=-1)
        c = c[None, :, None, :]
        s = s[None, :, None, :]
        return jnp.concatenate([u1 * c - u2 * s, u2 * c + u1 * s], axis=-1)

    xr, xc = jnp.split(xf, 2, axis=-1)
    return jnp.concatenate([rot(xr, cos_r, sin_r), rot(xc, cos_c, sin_c)], axis=-1).astype(x.dtype)


def gqa_attention(q, k, v):
    b, s = q.shape[0], q.shape[1]
    g = GQA_Q_HEADS // GQA_KV_HEADS
    nb = s // Q_BLOCK
    qb = q.reshape(b, nb, Q_BLOCK, GQA_KV_HEADS, g, HEAD_DIM).transpose(1, 0, 3, 4, 2, 5)
    kt = k.transpose(0, 2, 1, 3)
    vt = v.transpose(0, 2, 1, 3)
    scale = HEAD_DIM ** -0.5

    def block(qi):
        sc = jnp.einsum('bhgqd,bhkd->bhgqk', qi, kt, preferred_element_type=jnp.float32) * scale
        p = jax.nn.softmax(sc, axis=-1).astype(vt.dtype)
        return jnp.einsum('bhgqk,bhkd->bhgqd', p, vt)

    o = lax.map(block, qb)
    return o.transpose(1, 0, 4, 2, 3, 5).reshape(b, s, GQA_Q_HEADS * HEAD_DIM)


def neighbourhood_attention(q, k, v, rpb):
    b, s, h, d = q.shape
    rows = s // GRID_W
    kr = min(NA_ROWS_MAX, rows)

    def to_grid(u):
        return u.reshape(b, rows, GRID_W, h, d).transpose(0, 3, 1, 2, 4)

    qg, kg, vg = to_grid(q), to_grid(k), to_grid(v)
    cols = np.arange(GRID_W)
    col_start = np.clip(cols - NA_COLS // 2, 0, GRID_W - NA_COLS)
    col_idx = col_start[:, None] + np.arange(NA_COLS)[None, :]
    dc_idx = col_idx - cols[:, None] + NA_COLS - 1
    scale = d ** -0.5

    def row_block(r):
        rs = jnp.clip(r - kr // 2, 0, rows - kr)
        kb = lax.dynamic_slice_in_dim(kg, rs, kr, axis=2)[:, :, :, col_idx, :]
        vb = lax.dynamic_slice_in_dim(vg, rs, kr, axis=2)[:, :, :, col_idx, :]
        qr = lax.dynamic_index_in_dim(qg, r, axis=2, keepdims=False)
        sc = jnp.einsum('bhqd,bhrqkd->bhqrk', qr, kb, preferred_element_type=jnp.float32) * scale
        dr_idx = rs + jnp.arange(kr) - r + NA_ROWS_MAX - 1
        bias = rpb[:, dr_idx[:, None, None], dc_idx[None, :, :]]
        sc = sc + bias.transpose(0, 2, 1, 3)[None].astype(jnp.float32)
        p = jax.nn.softmax(sc.reshape(b, h, GRID_W, kr * NA_COLS), axis=-1)
        p = p.reshape(sc.shape).astype(vb.dtype)
        return jnp.einsum('bhqrk,bhrqkd->bhqd', p, vb)

    o = lax.map(row_block, jnp.arange(rows))
    return o.transpose(1, 0, 3, 2, 4).reshape(b, s, h * d)


def memory_attention(q, km, vm):
    b, s = q.shape[0], q.shape[1]
    scale = HEAD_DIM ** -0.5
    sc = jnp.einsum('bqhd,bkhd->bhqk', q, km, preferred_element_type=jnp.float32) * scale
    p = jax.nn.softmax(sc, axis=-1).astype(vm.dtype)
    return jnp.einsum('bhqk,bkhd->bqhd', p, vm).reshape(b, s, MEM_HEADS * HEAD_DIM)


def hybrid_mixer(h, mem, w_in, b_gate, q_norm_a, k_norm_a, na_rpb, w_mem_kv,
                 w_oa, w_ob, w_om, w_out):
    b, s, _ = h.shape
    z = h @ w_in
    qa, ka, va, qkvb, qm, gates = jnp.split(z, SPLITS, axis=-1)
    qa = head_rms_norm(qa.reshape(b, s, GQA_Q_HEADS, HEAD_DIM), q_norm_a)
    ka = head_rms_norm(ka.reshape(b, s, GQA_KV_HEADS, HEAD_DIM), k_norm_a)
    va = va.reshape(b, s, GQA_KV_HEADS, HEAD_DIM)
    tabs = axial_tables(s)
    qa = axial_rope(qa, *tabs)
    ka = axial_rope(ka, *tabs)
    oa = gqa_attention(qa, ka, va)
    qkvb = qkvb.reshape(b, s, 3, NA_HEADS, HEAD_DIM)
    ob = neighbourhood_attention(qkvb[:, :, 0], qkvb[:, :, 1], qkvb[:, :, 2], na_rpb)
    kvm = (mem @ w_mem_kv).reshape(b, mem.shape[1], 2, MEM_HEADS, HEAD_DIM)
    om = memory_attention(qm.reshape(b, s, MEM_HEADS, HEAD_DIM), kvm[:, :, 0], kvm[:, :, 1])
    g = jax.nn.sigmoid(gates + b_gate).reshape(b, s, N_BRANCHES, D_MODEL)
    y = g[:, :, 0] * (oa @ w_oa) + g[:, :, 1] * (ob @ w_ob) + g[:, :, 2] * (om @ w_om)
    return y @ w_out


def setup_inputs(seed: int = 0) -> dict:
    key = jax.random.key(seed)
    ks = jax.random.split(key, 24)
    L = DEPTH
    beta = DEEPNORM_BETA

    def nrm(k, shape, std):
        return jax.random.normal(k, shape, dtype=jnp.float32) * std

    def gain(k, shape):
        return 1.0 + nrm(k, shape, 0.01)

    return {
        "x": nrm(ks[0], (BATCH, SEQ, D_MODEL), 1.0),
        "mem": nrm(ks[1], (BATCH, MEM_LEN, D_MODEL), 1.0),
        "ln1_g": gain(ks[2], (L, D_MODEL)),
        "ln1_b": nrm(ks[3], (L, D_MODEL), 0.01),
        "ffn1_w_gu": nrm(ks[4], (L, D_MODEL, 2 * D_FF), beta * D_MODEL ** -0.5),
        "ffn1_w_down": nrm(ks[5], (L, D_FF, D_MODEL), beta * D_FF ** -0.5),
        "w_in": nrm(ks[6], (L, D_MODEL, W_IN_COLS), D_MODEL ** -0.5),
        "b_gate": nrm(ks[7], (L, N_BRANCHES * D_MODEL), 0.01),
        "q_norm_a": gain(ks[8], (L, HEAD_DIM)),
        "k_norm_a": gain(ks[9], (L, HEAD_DIM)),
        "na_rpb": nrm(ks[10], (L, NA_HEADS, 2 * NA_ROWS_MAX - 1, 2 * NA_COLS - 1), 0.1),
        "w_mem_kv": nrm(ks[11], (L, D_MODEL, 2 * WM), D_MODEL ** -0.5),
        "w_oa": nrm(ks[12], (L, WA_Q, D_MODEL), beta * WA_Q ** -0.5),
        "w_ob": nrm(ks[13], (L, WB, D_MODEL), beta * WB ** -0.5),
        "w_om": nrm(ks[14], (L, WM, D_MODEL), beta * WM ** -0.5),
        "w_out": nrm(ks[15], (L, D_MODEL, D_MODEL), beta * D_MODEL ** -0.5),
        "ln2_g": gain(ks[16], (L, D_MODEL)),
        "ln2_b": nrm(ks[17], (L, D_MODEL), 0.01),
        "ffn2_w_gu": nrm(ks[18], (L, D_MODEL, 2 * D_FF), beta * D_MODEL ** -0.5),
        "ffn2_w_down": nrm(ks[19], (L, D_FF, D_MODEL), beta * D_FF ** -0.5),
        "ln3_g": gain(ks[20], (L, D_MODEL)),
        "ln3_b": nrm(ks[21], (L, D_MODEL), 0.01),
    }


def reference(x, mem, ln1_g, ln1_b, ffn1_w_gu, ffn1_w_down, w_in, b_gate, q_norm_a,
              k_norm_a, na_rpb, w_mem_kv, w_oa, w_ob, w_om, w_out, ln2_g, ln2_b,
              ffn2_w_gu, ffn2_w_down, ln3_g, ln3_b):
    alpha = DEEPNORM_ALPHA
    for l in range(DEPTH):
        x = layer_norm(alpha * x + 0.5 * swiglu(x, ffn1_w_gu[l], ffn1_w_down[l]), ln1_g[l], ln1_b[l])
        mix = hybrid_mixer(x, mem, w_in[l], b_gate[l], q_norm_a[l], k_norm_a[l], na_rpb[l],
                           w_mem_kv[l], w_oa[l], w_ob[l], w_om[l], w_out[l])
        x = layer_norm(alpha * x + mix, ln2_g[l], ln2_b[l])
        x = layer_norm(alpha * x + 0.5 * swiglu(x, ffn2_w_gu[l], ffn2_w_down[l]), ln3_g[l], ln3_b[l])
    return x
```

```python
import functools

import numpy as np
import jax
import jax.numpy as jnp
from jax import lax
from jax.experimental import pallas as pl
from jax.experimental.pallas import tpu as pltpu

HEAD_DIM = 128
GQA_Q_HEADS = 8
GQA_KV_HEADS = 2
NA_HEADS = 4
MEM_HEADS = 4
GRID_W = 64
NA_ROWS_MAX = 8
NA_COLS = 16
ROPE_THETA = 10000.0
LN_EPS = 1e-5
RMS_EPS = 1e-6
N_BRANCHES = 3
MASK_VALUE = -1e30

WA_Q = GQA_Q_HEADS * HEAD_DIM
WA_KV = GQA_KV_HEADS * HEAD_DIM
WB = NA_HEADS * HEAD_DIM
WM = MEM_HEADS * HEAD_DIM
COL_KA = WA_Q
COL_VA = COL_KA + WA_KV
COL_QB = COL_VA + WA_KV
COL_KB = COL_QB + WB
COL_VB = COL_KB + WB
COL_QM = COL_VB + WB
W_ATTN_COLS = COL_QM + WM

V7X_VMEM_BYTES = 64 * 1024 * 1024
VMEM_LIMIT_BYTES = V7X_VMEM_BYTES - 8 * 1024 * 1024

BF16 = jnp.bfloat16
F32 = jnp.float32


def _params(*semantics):
    return pltpu.CompilerParams(dimension_semantics=semantics,
                                vmem_limit_bytes=VMEM_LIMIT_BYTES)


def _dot(a, b):
    return jnp.dot(a, b, preferred_element_type=F32)


def _dot_nt(a, b):
    return lax.dot_general(a, b, (((1,), (1,)), ((), ())), preferred_element_type=F32)


def _layer_norm_rows(y, g, b):
    mu = jnp.mean(y, axis=-1, keepdims=True)
    yc = y - mu
    var = jnp.mean(yc * yc, axis=-1, keepdims=True)
    return yc * lax.rsqrt(var + LN_EPS) * g + b


LN_ROW_CHUNK = 16


def _for_row_chunks(n_rows, body):
    def step(r, carry):
        body(pl.ds(pl.multiple_of(r * LN_ROW_CHUNK, LN_ROW_CHUNK), LN_ROW_CHUNK))
        return carry
    lax.fori_loop(0, n_rows // LN_ROW_CHUNK, step, 0)


def _ffn_ln_kernel(x_ref, wg_ref, wu_ref, wd_ref, g_ref, b_ref, o_ref, *rest,
                   alpha, emit_bf16):
    if emit_bf16:
        ob_ref, xb_ref = rest
    else:
        (xb_ref,) = rest
    j = pl.program_id(1)
    tm = x_ref.shape[0]

    @pl.when(j == 0)
    def _init():
        def body(rows):
            x = x_ref[rows, :]
            xb_ref[rows, :] = x.astype(BF16)
            o_ref[rows, :] = alpha * x
        _for_row_chunks(tm, body)

    xb = xb_ref[...]
    gate = _dot(xb, wg_ref[...])
    up = _dot(xb, wu_ref[...])
    h = (gate * jax.nn.sigmoid(gate) * (0.5 * up)).astype(BF16)
    o_ref[...] += _dot(h, wd_ref[...])

    @pl.when(j == pl.num_programs(1) - 1)
    def _finish():
        g = g_ref[...]
        b = b_ref[...]

        def body(rows):
            out = _layer_norm_rows(o_ref[rows, :], g, b)
            o_ref[rows, :] = out
            if emit_bf16:
                ob_ref[rows, :] = out.astype(BF16)
        _for_row_chunks(tm, body)


def _ffn_ln(x, w_gu, w_down, ln_g, ln_b, *, alpha, emit_bf16, tm=512, tf=512):
    t, d = x.shape
    f = w_down.shape[0]
    assert t % tm == 0 and f % tf == 0
    nf = f // tf
    out_shape = [jax.ShapeDtypeStruct((t, d), F32)]
    out_specs = [pl.BlockSpec((tm, d), lambda i, j: (i, 0))]
    if emit_bf16:
        out_shape.append(jax.ShapeDtypeStruct((t, d), BF16))
        out_specs.append(pl.BlockSpec((tm, d), lambda i, j: (i, 0)))
    res = pl.pallas_call(
        functools.partial(_ffn_ln_kernel, alpha=alpha, emit_bf16=emit_bf16),
        grid=(t // tm, nf),
        in_specs=[
            pl.BlockSpec((tm, d), lambda i, j: (i, 0)),
            pl.BlockSpec((d, tf), lambda i, j: (0, j)),
            pl.BlockSpec((d, tf), lambda i, j: (0, j + nf)),
            pl.BlockSpec((tf, d), lambda i, j: (j, 0)),
            pl.BlockSpec((1, d), lambda i, j: (0, 0)),
            pl.BlockSpec((1, d), lambda i, j: (0, 0)),
        ],
        out_specs=out_specs,
        out_shape=out_shape,
        scratch_shapes=[pltpu.VMEM((tm, d), BF16)],
        compiler_params=_params("parallel", "arbitrary"),
        name="ffn_ln",
    )(x, w_gu, w_gu, w_down, ln_g.reshape(1, d), ln_b.reshape(1, d))
    return res if emit_bf16 else res[0]


def _rope_tables(seq):
    t = np.arange(seq)
    pos = np.stack([t // GRID_W, t % GRID_W], axis=1).astype(np.float32)
    lane = np.arange(HEAD_DIM)
    half = lane // (HEAD_DIM // 2)
    within = lane % (HEAD_DIM // 2)
    quarter = HEAD_DIM // 4
    first = within < quarter
    freq = within % quarter
    axis_rot = HEAD_DIM // 2
    inv = jnp.asarray(ROPE_THETA, F32) ** (
        -jnp.arange(0, axis_rot, 2, dtype=F32) / axis_rot)
    ang = jnp.asarray(pos)[:, half] * inv[freq][None, :]
    cos = jnp.cos(ang)
    sin = jnp.sin(ang)
    first = jnp.asarray(first)[None, :]
    sin_up = jnp.where(first, -sin, 0.0)
    sin_dn = jnp.where(first, 0.0, sin)
    return cos, sin_up, sin_dn


def _inproj_kernel(xb_ref, w_ref, qg_ref, kg_ref, cos_ref, sup_ref, sdn_ref, o_ref):
    xb = xb_ref[...]
    cos = cos_ref[...]
    sup = sup_ref[...]
    sdn = sdn_ref[...]
    scale = HEAD_DIM ** -0.5
    quarter = HEAD_DIM // 4
    pair = 2 * HEAD_DIM

    def norm_rope(z, gain):
        z = z * lax.rsqrt(jnp.mean(z * z, axis=-1, keepdims=True) + RMS_EPS) * gain
        return (z * cos
                + pltpu.roll(z, HEAD_DIM - quarter, 1) * sup
                + pltpu.roll(z, quarter, 1) * sdn)

    for c in range(0, W_ATTN_COLS, pair):
        z2 = _dot(xb, w_ref[:, c:c + pair])
        for hh in range(2):
            col = c + hh * HEAD_DIM
            z = z2[:, hh * HEAD_DIM:(hh + 1) * HEAD_DIM]
            if col < COL_KA:
                z = norm_rope(z, qg_ref[...]) * scale
            elif col < COL_VA:
                z = norm_rope(z, kg_ref[...])
            elif COL_QB <= col < COL_KB or col >= COL_QM:
                z = z * scale
            o_ref[:, col:col + HEAD_DIM] = z.astype(BF16)


def _inproj(xb, w_in, q_gain, k_gain, seq, *, tm=512):
    t, d = xb.shape
    assert t % tm == 0 and seq % tm == 0
    cos, sup, sdn = _rope_tables(seq)
    nseq = seq // tm
    tab_spec = pl.BlockSpec((tm, HEAD_DIM), lambda i: (i % nseq, 0))
    gain_spec = pl.BlockSpec((1, HEAD_DIM), lambda i: (0, 0))
    return pl.pallas_call(
        _inproj_kernel,
        grid=(t // tm,),
        in_specs=[
            pl.BlockSpec((tm, d), lambda i: (i, 0)),
            pl.BlockSpec((d, W_ATTN_COLS), lambda i: (0, 0)),
            gain_spec, gain_spec, tab_spec, tab_spec, tab_spec,
        ],
        out_specs=pl.BlockSpec((tm, W_ATTN_COLS), lambda i: (i, 0)),
        out_shape=jax.ShapeDtypeStruct((t, W_ATTN_COLS), BF16),
        compiler_params=_params("parallel"),
        name="inproj",
    )(xb, w_in, q_gain.reshape(1, HEAD_DIM), k_gain.reshape(1, HEAD_DIM), cos, sup, sdn)


def _memkv_kernel(m_ref, w_ref, o_ref):
    o_ref[...] = _dot(m_ref[...].astype(BF16), w_ref[...]).astype(BF16)


def _memkv(mem2d, w, *, tm=512):
    t, d = mem2d.shape
    n = w.shape[1]
    assert t % tm == 0
    return pl.pallas_call(
        _memkv_kernel,
        grid=(t // tm,),
        in_specs=[pl.BlockSpec((tm, d), lambda i: (i, 0)),
                  pl.BlockSpec((d, n), lambda i: (0, 0))],
        out_specs=pl.BlockSpec((tm, n), lambda i: (i, 0)),
        out_shape=jax.ShapeDtypeStruct((t, n), BF16),
        compiler_params=_params("parallel"),
        name="memkv",
    )(mem2d, w)


def _softmax_pv(s, v):
    m = jnp.max(s, axis=-1, keepdims=True)
    p = jnp.exp(s - m)
    l = jnp.sum(p, axis=-1, keepdims=True)
    return _dot(p.astype(BF16), v) / l


def _gqa_kernel(q_ref, k_ref, v_ref, o_ref):
    s = _dot_nt(q_ref[...], k_ref[...])
    o_ref[...] = _softmax_pv(s, v_ref[...]).astype(BF16)


def _gqa(z, batch, seq, *, tq=512):
    t = z.shape[0]
    nq = seq // tq
    group = GQA_Q_HEADS // GQA_KV_HEADS
    kcol = COL_KA // HEAD_DIM
    vcol = COL_VA // HEAD_DIM
    return pl.pallas_call(
        _gqa_kernel,
        grid=(batch, GQA_Q_HEADS, nq),
        in_specs=[
            pl.BlockSpec((tq, HEAD_DIM), lambda b, h, i: (b * nq + i, h)),
            pl.BlockSpec((seq, HEAD_DIM), lambda b, h, i: (b, kcol + h // group)),
            pl.BlockSpec((seq, HEAD_DIM), lambda b, h, i: (b, vcol + h // group)),
        ],
        out_specs=pl.BlockSpec((tq, HEAD_DIM), lambda b, h, i: (b * nq + i, h)),
        out_shape=jax.ShapeDtypeStruct((t, WA_Q), BF16),
        compiler_params=_params("parallel", "parallel", "parallel"),
        name="gqa",
    )(z, z, z)


NA_Q_ROWS = 4
NA_K_ROWS = NA_Q_ROWS + NA_ROWS_MAX
NA_TQ = NA_Q_ROWS * GRID_W
NA_TK = NA_K_ROWS * GRID_W


def _na_key_row_start(qblock, rows):
    return np.clip(qblock * NA_Q_ROWS - NA_ROWS_MAX // 2, 0, rows - NA_K_ROWS)


def _na_bias_tables(rpb, rows):
    nblocks = rows // NA_Q_ROWS
    kr_win = min(NA_ROWS_MAX, rows)
    tabs = []
    for qblock in (0, 1, nblocks - 1):
        ks = _na_key_row_start(qblock, rows)
        tq = np.arange(NA_TQ)
        tk = np.arange(NA_TK)
        qr = qblock * NA_Q_ROWS + tq // GRID_W
        qc = tq % GRID_W
        kr = ks + tk // GRID_W
        kc = tk % GRID_W
        rs = np.clip(qr - kr_win // 2, 0, rows - kr_win)
        cs = np.clip(qc - NA_COLS // 2, 0, GRID_W - NA_COLS)
        ok_r = (kr[None, :] >= rs[:, None]) & (kr[None, :] < rs[:, None] + kr_win)
        ok_c = (kc[None, :] >= cs[:, None]) & (kc[None, :] < cs[:, None] + NA_COLS)
        ok = ok_r & ok_c
        dr = np.where(ok, kr[None, :] - qr[:, None] + NA_ROWS_MAX - 1, 0)
        dc = np.where(ok, kc[None, :] - qc[:, None] + NA_COLS - 1, 0)
        bias = rpb.astype(F32)[:, dr, dc]
        tabs.append(jnp.where(jnp.asarray(ok)[None], bias, MASK_VALUE))
    return jnp.stack(tabs)


def _na_kernel(q_ref, k_ref, v_ref, bias_ref, o_ref, *, n_blocks):
    i = pl.program_id(1)
    kblock = jnp.clip(i - NA_ROWS_MAX // 2 // NA_Q_ROWS, 0, n_blocks - NA_K_ROWS // NA_Q_ROWS)
    start = pl.multiple_of(kblock * NA_TQ, NA_TQ)
    for h in range(NA_HEADS):
        cols = slice(h * HEAD_DIM, (h + 1) * HEAD_DIM)
        k = k_ref[pl.ds(start, NA_TK), cols]
        v = v_ref[pl.ds(start, NA_TK), cols]
        s = _dot_nt(q_ref[:, cols], k) + bias_ref[0, h]
        o_ref[:, cols] = _softmax_pv(s, v).astype(BF16)


def _na(z, rpb, batch, seq):
    t = z.shape[0]
    rows = seq // GRID_W
    assert rows % NA_Q_ROWS == 0 and rows >= NA_K_ROWS and NA_ROWS_MAX % (2 * NA_Q_ROWS) == 0
    n_blocks = rows // NA_Q_ROWS
    bias = _na_bias_tables(rpb, rows)

    def pattern(b, i):
        return jnp.where(i == 0, 0, jnp.where(i == n_blocks - 1, 2, 1))

    return pl.pallas_call(
        functools.partial(_na_kernel, n_blocks=n_blocks),
        grid=(batch, n_blocks),
        in_specs=[
            pl.BlockSpec((NA_TQ, WB), lambda b, i: (b * n_blocks + i, COL_QB // WB)),
            pl.BlockSpec((seq, WB), lambda b, i: (b, COL_KB // WB)),
            pl.BlockSpec((seq, WB), lambda b, i: (b, COL_VB // WB)),
            pl.BlockSpec((1, NA_HEADS, NA_TQ, NA_TK), lambda b, i: (pattern(b, i), 0, 0, 0)),
        ],
        out_specs=pl.BlockSpec((NA_TQ, WB), lambda b, i: (b * n_blocks + i, 0)),
        out_shape=jax.ShapeDtypeStruct((t, WB), BF16),
        compiler_params=_params("parallel", "arbitrary"),
        name="natten",
    )(z, z, z, bias)


def _mem_attn_kernel(q_ref, kv_ref, o_ref):
    for h in range(MEM_HEADS):
        cols = slice(h * HEAD_DIM, (h + 1) * HEAD_DIM)
        s = _dot_nt(q_ref[:, cols], kv_ref[:, cols])
        v = kv_ref[:, WM + h * HEAD_DIM:WM + (h + 1) * HEAD_DIM]
        o_ref[:, cols] = _softmax_pv(s, v).astype(BF16)


def _mem_attn(z, kvm, batch, seq, mem_len, *, tq=512):
    t = z.shape[0]
    nq = seq // tq
    return pl.pallas_call(
        _mem_attn_kernel,
        grid=(batch, nq),
        in_specs=[
            pl.BlockSpec((tq, WM), lambda b, i: (b * nq + i, COL_QM // WM)),
            pl.BlockSpec((mem_len, 2 * WM), lambda b, i: (b, 0)),
        ],
        out_specs=pl.BlockSpec((tq, WM), lambda b, i: (b * nq + i, 0)),
        out_shape=jax.ShapeDtypeStruct((t, WM), BF16),
        compiler_params=_params("parallel", "parallel"),
        name="mem_attn",
    )(z, kvm)


def _merge_ln_kernel(x_ref, xb_ref, oa_ref, ob_ref, om_ref, wg0_ref, wg1_ref, wg2_ref,
                     bg_ref, woa_ref, wob_ref, wom_ref, wout_ref, g_ref, b_ref, o_ref,
                     *, alpha):
    c = pl.program_id(1)
    tm = x_ref.shape[0]

    @pl.when(c == 0)
    def _init():
        def body(rows):
            o_ref[rows, :] = alpha * x_ref[rows, :]
        _for_row_chunks(tm, body)

    xb = xb_ref[...]
    bg = bg_ref[...]

    def gated(k, wg_ref, br_ref, wo_ref):
        gate = jax.nn.sigmoid(_dot(xb, wg_ref[...]) + bg[k:k + 1, :])
        return gate * _dot(br_ref[...], wo_ref[...])

    y = (gated(0, wg0_ref, oa_ref, woa_ref)
         + gated(1, wg1_ref, ob_ref, wob_ref)
         + gated(2, wg2_ref, om_ref, wom_ref))
    o_ref[...] += _dot(y.astype(BF16), wout_ref[...])

    @pl.when(c == pl.num_programs(1) - 1)
    def _finish():
        g = g_ref[...]
        b = b_ref[...]

        def body(rows):
            o_ref[rows, :] = _layer_norm_rows(o_ref[rows, :], g, b)
        _for_row_chunks(tm, body)


def _merge_ln(x, xb, oa, ob, om, w_in, b_gate, w_oa, w_ob, w_om, w_out, ln_g, ln_b,
              *, alpha, tm=512, tn=256):
    t, d = x.shape
    assert t % tm == 0 and d % tn == 0 and W_ATTN_COLS % tn == 0
    nc = d // tn
    gate0 = W_ATTN_COLS // tn

    def rows(width):
        return pl.BlockSpec((tm, width), lambda i, c: (i, 0))

    def gate_w(k):
        return pl.BlockSpec((d, tn), lambda i, c: (0, gate0 + k * nc + c))

    def branch_w(width):
        return pl.BlockSpec((width, tn), lambda i, c: (0, c))

    vec = pl.BlockSpec((1, d), lambda i, c: (0, 0))
    return pl.pallas_call(
        functools.partial(_merge_ln_kernel, alpha=alpha),
        grid=(t // tm, nc),
        in_specs=[
            rows(d), rows(d), rows(WA_Q), rows(WB), rows(WM),
            gate_w(0), gate_w(1), gate_w(2),
            pl.BlockSpec((N_BRANCHES, tn), lambda i, c: (0, c)),
            branch_w(WA_Q), branch_w(WB), branch_w(WM),
            pl.BlockSpec((tn, d), lambda i, c: (c, 0)),
            vec, vec,
        ],
        out_specs=rows(d),
        out_shape=jax.ShapeDtypeStruct((t, d), F32),
        compiler_params=_params("parallel", "arbitrary"),
        name="merge_ln",
    )(x, xb, oa, ob, om, w_in, w_in, w_in, b_gate.reshape(N_BRANCHES, d),
      w_oa, w_ob, w_om, w_out, ln_g.reshape(1, d), ln_b.reshape(1, d))


def kernel(x, mem, ln1_g, ln1_b, ffn1_w_gu, ffn1_w_down, w_in, b_gate, q_norm_a, k_norm_a,
           na_rpb, w_mem_kv, w_oa, w_ob, w_om, w_out, ln2_g, ln2_b, ffn2_w_gu, ffn2_w_down,
           ln3_g, ln3_b):
    batch, seq, d = x.shape
    mem_len = mem.shape[1]
    depth = w_in.shape[0]
    alpha = (2 * depth) ** 0.25
    h = x.reshape(batch * seq, d)
    mem2d = mem.reshape(batch * mem_len, d)
    for l in range(depth):
        bf = lambda w: w[l].astype(BF16)
        h, hb = _ffn_ln(h, bf(ffn1_w_gu), bf(ffn1_w_down), ln1_g[l], ln1_b[l],
                        alpha=alpha, emit_bf16=True)
        w_in_b = bf(w_in)
        z = _inproj(hb, w_in_b, q_norm_a[l], k_norm_a[l], seq)
        kvm = _memkv(mem2d, bf(w_mem_kv))
        oa = _gqa(z, batch, seq)
        ob = _na(z, na_rpb[l], batch, seq)
        om = _mem_attn(z, kvm, batch, seq, mem_len)
        h = _merge_ln(h, hb, oa, ob, om, w_in_b, b_gate[l], bf(w_oa), bf(w_ob), bf(w_om),
                      bf(w_out), ln2_g[l], ln2_b[l], alpha=alpha)
        h = _ffn_ln(h, bf(ffn2_w_gu), bf(ffn2_w_down), ln3_g[l], ln3_b[l],
                    alpha=alpha, emit_bf16=False)
    return h.reshape(batch, seq, d)
```

```python
import functools

import numpy as np
import jax
import jax.numpy as jnp
from jax import lax
from jax.experimental import pallas as pl
from jax.experimental.pallas import tpu as pltpu

HEAD_DIM = 128
GQA_Q_HEADS = 8
GQA_KV_HEADS = 2
NA_HEADS = 4
MEM_HEADS = 4
GRID_W = 64
NA_ROWS_MAX = 8
NA_COLS = 16
ROPE_THETA = 10000.0
LN_EPS = 1e-5
RMS_EPS = 1e-6
N_BRANCHES = 3
MASK_VALUE = -1e30

WA_Q = GQA_Q_HEADS * HEAD_DIM
WA_KV = GQA_KV_HEADS * HEAD_DIM
WB = NA_HEADS * HEAD_DIM
WM = MEM_HEADS * HEAD_DIM
COL_KA = WA_Q
COL_VA = COL_KA + WA_KV
COL_QB = COL_VA + WA_KV
COL_KB = COL_QB + WB
COL_VB = COL_KB + WB
COL_QM = COL_VB + WB
W_ATTN_COLS = COL_QM + WM

V7X_VMEM_BYTES = 64 * 1024 * 1024
VMEM_LIMIT_BYTES = V7X_VMEM_BYTES - 8 * 1024 * 1024

BF16 = jnp.bfloat16
F32 = jnp.float32


def _params(*semantics):
    return pltpu.CompilerParams(dimension_semantics=semantics,
                                vmem_limit_bytes=VMEM_LIMIT_BYTES)


def _dot(a, b):
    return jnp.dot(a, b, preferred_element_type=F32)


def _dot_nt(a, b):
    return lax.dot_general(a, b, (((1,), (1,)), ((), ())), preferred_element_type=F32)


def _layer_norm_rows(y, g, b):
    mu = jnp.mean(y, axis=-1, keepdims=True)
    yc = y - mu
    var = jnp.mean(yc * yc, axis=-1, keepdims=True)
    return yc * lax.rsqrt(var + LN_EPS) * g + b


LN_ROW_CHUNK = 128


def _for_row_chunks(n_rows, body):
    def step(r, carry):
        body(pl.ds(pl.multiple_of(r * LN_ROW_CHUNK, LN_ROW_CHUNK), LN_ROW_CHUNK))
        return carry
    lax.fori_loop(0, n_rows // LN_ROW_CHUNK, step, 0)


def _ffn_ln_kernel(x_ref, wg_ref, wu_ref, wd_ref, g_ref, b_ref, o_ref, xb_ref, *, alpha):
    j = pl.program_id(1)
    tm = x_ref.shape[0]

    @pl.when(j == 0)
    def _init():
        def body(rows):
            x = x_ref[rows, :]
            xb_ref[rows, :] = x.astype(BF16)
            o_ref[rows, :] = alpha * x
        _for_row_chunks(tm, body)

    xb = xb_ref[...]
    gate = _dot(xb, wg_ref[...])
    up = _dot(xb, wu_ref[...])
    h = (gate * jax.nn.sigmoid(gate) * (0.5 * up)).astype(BF16)
    o_ref[...] += _dot(h, wd_ref[...])

    @pl.when(j == pl.num_programs(1) - 1)
    def _finish():
        g = g_ref[...]
        b = b_ref[...]

        def body(rows):
            o_ref[rows, :] = _layer_norm_rows(o_ref[rows, :], g, b)
        _for_row_chunks(tm, body)


def _ffn_ln(x, w_gu, w_down, ln_g, ln_b, *, alpha, tm=1024, tf=512):
    t, d = x.shape
    f = w_down.shape[0]
    assert t % tm == 0 and f % tf == 0
    nf = f // tf
    return pl.pallas_call(
        functools.partial(_ffn_ln_kernel, alpha=alpha),
        grid=(t // tm, nf),
        in_specs=[
            pl.BlockSpec((tm, d), lambda i, j: (i, 0), pipeline_mode=pl.Buffered(1)),
            pl.BlockSpec((d, tf), lambda i, j: (0, j)),
            pl.BlockSpec((d, tf), lambda i, j: (0, j + nf)),
            pl.BlockSpec((tf, d), lambda i, j: (j, 0)),
            pl.BlockSpec((1, d), lambda i, j: (0, 0)),
            pl.BlockSpec((1, d), lambda i, j: (0, 0)),
        ],
        out_specs=pl.BlockSpec((tm, d), lambda i, j: (i, 0)),
        out_shape=jax.ShapeDtypeStruct((t, d), F32),
        scratch_shapes=[pltpu.VMEM((tm, d), BF16)],
        compiler_params=_params("parallel", "arbitrary"),
        name="ffn_ln",
    )(x, w_gu, w_gu, w_down, ln_g.reshape(1, d), ln_b.reshape(1, d))


def _rope_tables(seq):
    t = np.arange(seq)
    pos = np.stack([t // GRID_W, t % GRID_W], axis=1).astype(np.float32)
    lane = np.arange(HEAD_DIM)
    half = lane // (HEAD_DIM // 2)
    within = lane % (HEAD_DIM // 2)
    quarter = HEAD_DIM // 4
    first = within < quarter
    freq = within % quarter
    axis_rot = HEAD_DIM // 2
    inv = jnp.asarray(ROPE_THETA, F32) ** (
        -jnp.arange(0, axis_rot, 2, dtype=F32) / axis_rot)
    ang = jnp.asarray(pos)[:, half] * inv[freq][None, :]
    cos = jnp.cos(ang)
    sin = jnp.sin(ang)
    first = jnp.asarray(first)[None, :]
    sin_up = jnp.where(first, -sin, 0.0)
    sin_dn = jnp.where(first, 0.0, sin)
    return cos, sin_up, sin_dn


def _inproj_kernel(x_ref, w_ref, qg_ref, kg_ref, cos_ref, sup_ref, sdn_ref, o_ref):
    xb = x_ref[...].astype(BF16)
    cos = cos_ref[...]
    sup = sup_ref[...]
    sdn = sdn_ref[...]
    scale = HEAD_DIM ** -0.5
    quarter = HEAD_DIM // 4
    pair = 2 * HEAD_DIM

    def norm_rope(z, gain):
        z = z * lax.rsqrt(jnp.mean(z * z, axis=-1, keepdims=True) + RMS_EPS) * gain
        return (z * cos
                + pltpu.roll(z, HEAD_DIM - quarter, 1) * sup
                + pltpu.roll(z, quarter, 1) * sdn)

    for c in range(0, W_ATTN_COLS, pair):
        z2 = _dot(xb, w_ref[:, c:c + pair])
        for hh in range(2):
            col = c + hh * HEAD_DIM
            z = z2[:, hh * HEAD_DIM:(hh + 1) * HEAD_DIM]
            if col < COL_KA:
                z = norm_rope(z, qg_ref[...]) * scale
            elif col < COL_VA:
                z = norm_rope(z, kg_ref[...])
            elif COL_QB <= col < COL_KB or col >= COL_QM:
                z = z * scale
            o_ref[:, col:col + HEAD_DIM] = z.astype(BF16)


def _inproj(x, w_in, q_gain, k_gain, seq, *, tm=512):
    t, d = x.shape
    assert t % tm == 0 and seq % tm == 0
    cos, sup, sdn = _rope_tables(seq)
    nseq = seq // tm
    tab_spec = pl.BlockSpec((tm, HEAD_DIM), lambda i: (i % nseq, 0))
    gain_spec = pl.BlockSpec((1, HEAD_DIM), lambda i: (0, 0))
    return pl.pallas_call(
        _inproj_kernel,
        grid=(t // tm,),
        in_specs=[
            pl.BlockSpec((tm, d), lambda i: (i, 0)),
            pl.BlockSpec((d, W_ATTN_COLS), lambda i: (0, 0), pipeline_mode=pl.Buffered(1)),
            gain_spec, gain_spec, tab_spec, tab_spec, tab_spec,
        ],
        out_specs=pl.BlockSpec((tm, W_ATTN_COLS), lambda i: (i, 0)),
        out_shape=jax.ShapeDtypeStruct((t, W_ATTN_COLS), BF16),
        compiler_params=_params("parallel"),
        name="inproj",
    )(x, w_in, q_gain.reshape(1, HEAD_DIM), k_gain.reshape(1, HEAD_DIM), cos, sup, sdn)


def _memkv_kernel(m_ref, w_ref, o_ref):
    o_ref[...] = _dot(m_ref[...].astype(BF16), w_ref[...]).astype(BF16)


def _memkv(mem2d, w, *, tm=512):
    t, d = mem2d.shape
    n = w.shape[1]
    assert t % tm == 0
    return pl.pallas_call(
        _memkv_kernel,
        grid=(t // tm,),
        in_specs=[pl.BlockSpec((tm, d), lambda i: (i, 0)),
                  pl.BlockSpec((d, n), lambda i: (0, 0))],
        out_specs=pl.BlockSpec((tm, n), lambda i: (i, 0)),
        out_shape=jax.ShapeDtypeStruct((t, n), BF16),
        compiler_params=_params("parallel"),
        name="memkv",
    )(mem2d, w)


def _softmax_pv(s, v):
    m = jnp.max(s, axis=-1, keepdims=True)
    p = jnp.exp(s - m)
    l = jnp.sum(p, axis=-1, keepdims=True)
    return _dot(p.astype(BF16), v) / l


GQA_GROUP = GQA_Q_HEADS // GQA_KV_HEADS


def _gqa_kernel(q_ref, k_ref, v_ref, o_ref):
    for h in range(GQA_GROUP):
        cols = slice(h * HEAD_DIM, (h + 1) * HEAD_DIM)
        s = _dot_nt(q_ref[:, cols], k_ref[...])
        o_ref[:, cols] = _softmax_pv(s, v_ref[...]).astype(BF16)


def _gqa(z, batch, seq, *, tq=256):
    t = z.shape[0]
    nq = seq // tq
    gw = GQA_GROUP * HEAD_DIM
    kcol = COL_KA // HEAD_DIM
    vcol = COL_VA // HEAD_DIM
    return pl.pallas_call(
        _gqa_kernel,
        grid=(batch, GQA_KV_HEADS, nq),
        in_specs=[
            pl.BlockSpec((tq, gw), lambda b, g, i: (b * nq + i, g)),
            pl.BlockSpec((seq, HEAD_DIM), lambda b, g, i: (b, kcol + g)),
            pl.BlockSpec((seq, HEAD_DIM), lambda b, g, i: (b, vcol + g)),
        ],
        out_specs=pl.BlockSpec((tq, gw), lambda b, g, i: (b * nq + i, g)),
        out_shape=jax.ShapeDtypeStruct((t, WA_Q), BF16),
        compiler_params=_params("parallel", "parallel", "parallel"),
        name="gqa",
    )(z, z, z)


NA_Q_ROWS = 4
NA_K_ROWS = NA_Q_ROWS + NA_ROWS_MAX
NA_TQ = NA_Q_ROWS * GRID_W
NA_TK = NA_K_ROWS * GRID_W


def _na_key_row_start(qblock, rows):
    return np.clip(qblock * NA_Q_ROWS - NA_ROWS_MAX // 2, 0, rows - NA_K_ROWS)


def _na_bias_tables(rpb, rows):
    heads, n_dr, n_dc = rpb.shape
    nblocks = rows // NA_Q_ROWS
    kr_win = min(NA_ROWS_MAX, rows)
    period = 2 * GRID_W
    assert period >= GRID_W + NA_COLS and n_dc == 2 * NA_COLS - 1
    v = rpb.astype(F32)
    vpad = jnp.concatenate(
        [v[..., NA_COLS - 1:], jnp.zeros((heads, n_dr, period - n_dc), F32), v[..., :NA_COLS - 1]],
        axis=-1)
    skew = jnp.tile(vpad, (1, 1, GRID_W))[..., :GRID_W * (period - 1)]
    toep = skew.reshape(heads, n_dr, GRID_W, period - 1)[..., :GRID_W]
    qc = np.arange(GRID_W)
    cs = np.clip(qc - NA_COLS // 2, 0, GRID_W - NA_COLS)
    ok_c = (qc[None, :] >= cs[:, None]) & (qc[None, :] < cs[:, None] + NA_COLS)
    toep = jnp.where(jnp.asarray(ok_c)[None, None], toep, MASK_VALUE)
    masked = jnp.full((heads, GRID_W, GRID_W), MASK_VALUE, F32)
    tabs = []
    for qblock in (0, 1, nblocks - 1):
        ks = _na_key_row_start(qblock, rows)
        q_rows = []
        for qi in range(NA_Q_ROWS):
            qr = qblock * NA_Q_ROWS + qi
            rs = np.clip(qr - kr_win // 2, 0, rows - kr_win)
            blocks = []
            for kj in range(NA_K_ROWS):
                kr = ks + kj
                inside = rs <= kr < rs + kr_win
                blocks.append(toep[:, kr - qr + NA_ROWS_MAX - 1] if inside else masked)
            q_rows.append(jnp.concatenate(blocks, axis=-1))
        tabs.append(jnp.concatenate(q_rows, axis=-2))
    return jnp.stack(tabs)


def _na_kernel(q_ref, k_ref, v_ref, bias_ref, o_ref, *, n_blocks):
    i = pl.program_id(1)
    kblock = jnp.clip(i - NA_ROWS_MAX // 2 // NA_Q_ROWS, 0, n_blocks - NA_K_ROWS // NA_Q_ROWS)
    start = pl.multiple_of(kblock * NA_TQ, NA_TQ)
    for h in range(NA_HEADS):
        cols = slice(h * HEAD_DIM, (h + 1) * HEAD_DIM)
        k = k_ref[pl.ds(start, NA_TK), cols]
        v = v_ref[pl.ds(start, NA_TK), cols]
        s = _dot_nt(q_ref[:, cols], k) + bias_ref[0, h]
        o_ref[:, cols] = _softmax_pv(s, v).astype(BF16)


def _na(z, rpb, batch, seq):
    t = z.shape[0]
    rows = seq // GRID_W
    assert rows % NA_Q_ROWS == 0 and rows >= NA_K_ROWS and NA_ROWS_MAX % (2 * NA_Q_ROWS) == 0
    n_blocks = rows // NA_Q_ROWS
    bias = _na_bias_tables(rpb, rows)

    def pattern(b, i):
        return jnp.where(i == 0, 0, jnp.where(i == n_blocks - 1, 2, 1))

    return pl.pallas_call(
        functools.partial(_na_kernel, n_blocks=n_blocks),
        grid=(batch, n_blocks),
        in_specs=[
            pl.BlockSpec((NA_TQ, WB), lambda b, i: (b * n_blocks + i, COL_QB // WB)),
            pl.BlockSpec((seq, WB), lambda b, i: (b, COL_KB // WB)),
            pl.BlockSpec((seq, WB), lambda b, i: (b, COL_VB // WB)),
            pl.BlockSpec((1, NA_HEADS, NA_TQ, NA_TK), lambda b, i: (pattern(b, i), 0, 0, 0)),
        ],
        out_specs=pl.BlockSpec((NA_TQ, WB), lambda b, i: (b * n_blocks + i, 0)),
        out_shape=jax.ShapeDtypeStruct((t, WB), BF16),
        compiler_params=_params("parallel", "arbitrary"),
        name="natten",
    )(z, z, z, bias)


def _mem_attn_kernel(q_ref, kv_ref, o_ref):
    for h in range(MEM_HEADS):
        cols = slice(h * HEAD_DIM, (h + 1) * HEAD_DIM)
        s = _dot_nt(q_ref[:, cols], kv_ref[:, cols])
        v = kv_ref[:, WM + h * HEAD_DIM:WM + (h + 1) * HEAD_DIM]
        o_ref[:, cols] = _softmax_pv(s, v).astype(BF16)


def _mem_attn(z, kvm, batch, seq, mem_len, *, tq=512):
    t = z.shape[0]
    nq = seq // tq
    return pl.pallas_call(
        _mem_attn_kernel,
        grid=(batch, nq),
        in_specs=[
            pl.BlockSpec((tq, WM), lambda b, i: (b * nq + i, COL_QM // WM)),
            pl.BlockSpec((mem_len, 2 * WM), lambda b, i: (b, 0)),
        ],
        out_specs=pl.BlockSpec((tq, WM), lambda b, i: (b * nq + i, 0)),
        out_shape=jax.ShapeDtypeStruct((t, WM), BF16),
        compiler_params=_params("parallel", "parallel"),
        name="mem_attn",
    )(z, kvm)


def _merge_ln_kernel(x_ref, oa_ref, ob_ref, om_ref, wg0_ref, wg1_ref, wg2_ref,
                     bg_ref, woa_ref, wob_ref, wom_ref, wout_ref, g_ref, b_ref, o_ref,
                     xb_ref, *, alpha):
    c = pl.program_id(1)
    tm = x_ref.shape[0]

    @pl.when(c == 0)
    def _init():
        def body(rows):
            x = x_ref[rows, :]
            xb_ref[rows, :] = x.astype(BF16)
            o_ref[rows, :] = alpha * x
        _for_row_chunks(tm, body)

    xb = xb_ref[...]
    bg = bg_ref[...]

    def gated(k, wg_ref, br_ref, wo_ref):
        gate = jax.nn.sigmoid(_dot(xb, wg_ref[...]) + bg[k:k + 1, :])
        return gate * _dot(br_ref[...], wo_ref[...])

    y = (gated(0, wg0_ref, oa_ref, woa_ref)
         + gated(1, wg1_ref, ob_ref, wob_ref)
         + gated(2, wg2_ref, om_ref, wom_ref))
    o_ref[...] += _dot(y.astype(BF16), wout_ref[...])

    @pl.when(c == pl.num_programs(1) - 1)
    def _finish():
        g = g_ref[...]
        b = b_ref[...]

        def body(rows):
            o_ref[rows, :] = _layer_norm_rows(o_ref[rows, :], g, b)
        _for_row_chunks(tm, body)


def _merge_ln(x, oa, ob, om, w_in, b_gate, w_oa, w_ob, w_om, w_out, ln_g, ln_b,
              *, alpha, tm=512, tn=256):
    t, d = x.shape
    assert t % tm == 0 and d % tn == 0 and W_ATTN_COLS % tn == 0
    nc = d // tn
    gate0 = W_ATTN_COLS // tn

    def rows(width):
        return pl.BlockSpec((tm, width), lambda i, c: (i, 0))

    def gate_w(k):
        return pl.BlockSpec((d, tn), lambda i, c: (0, gate0 + k * nc + c))

    def branch_w(width):
        return pl.BlockSpec((width, tn), lambda i, c: (0, c))

    vec = pl.BlockSpec((1, d), lambda i, c: (0, 0))
    return pl.pallas_call(
        functools.partial(_merge_ln_kernel, alpha=alpha),
        grid=(t // tm, nc),
        in_specs=[
            rows(d), rows(WA_Q), rows(WB), rows(WM),
            gate_w(0), gate_w(1), gate_w(2),
            pl.BlockSpec((N_BRANCHES, tn), lambda i, c: (0, c)),
            branch_w(WA_Q), branch_w(WB), branch_w(WM),
            pl.BlockSpec((tn, d), lambda i, c: (c, 0)),
            vec, vec,
        ],
        out_specs=rows(d),
        out_shape=jax.ShapeDtypeStruct((t, d), F32),
        scratch_shapes=[pltpu.VMEM((tm, d), BF16)],
        compiler_params=_params("parallel", "arbitrary"),
        name="merge_ln",
    )(x, oa, ob, om, w_in, w_in, w_in, b_gate.reshape(N_BRANCHES, d),
      w_oa, w_ob, w_om, w_out, ln_g.reshape(1, d), ln_b.reshape(1, d))


def kernel(x, mem, ln1_g, ln1_b, ffn1_w_gu, ffn1_w_down, w_in, b_gate, q_norm_a, k_norm_a,
           na_rpb, w_mem_kv, w_oa, w_ob, w_om, w_out, ln2_g, ln2_b, ffn2_w_gu, ffn2_w_down,
           ln3_g, ln3_b):
    batch, seq, d = x.shape
    mem_len = mem.shape[1]
    depth = w_in.shape[0]
    alpha = (2 * depth) ** 0.25
    h = x.reshape(batch * seq, d)
    mem2d = mem.reshape(batch * mem_len, d)
    for l in range(depth):
        bf = lambda w: w[l].astype(BF16)
        h = _ffn_ln(h, bf(ffn1_w_gu), bf(ffn1_w_down), ln1_g[l], ln1_b[l], alpha=alpha)
        w_in_b = bf(w_in)
        z = _inproj(h, w_in_b, q_norm_a[l], k_norm_a[l], seq)
        kvm = _memkv(mem2d, bf(w_mem_kv))
        oa = _gqa(z, batch, seq)
        ob = _na(z, na_rpb[l], batch, seq)
        om = _mem_attn(z, kvm, batch, seq, mem_len)
        h = _merge_ln(h, oa, ob, om, w_in_b, b_gate[l], bf(w_oa), bf(w_ob), bf(w_om),
                      bf(w_out), ln2_g[l], ln2_b[l], alpha=alpha)
        h = _ffn_ln(h, bf(ffn2_w_gu), bf(ffn2_w_down), ln3_g[l], ln3_b[l], alpha=alpha)
    return h.reshape(batch, seq, d)
```

```python
import functools

import numpy as np
import jax
import jax.numpy as jnp
from jax import lax
from jax.experimental import pallas as pl
from jax.experimental.pallas import tpu as pltpu

HEAD_DIM = 128
GQA_Q_HEADS = 8
GQA_KV_HEADS = 2
NA_HEADS = 4
MEM_HEADS = 4
GRID_W = 64
NA_ROWS_MAX = 8
NA_COLS = 16
ROPE_THETA = 10000.0
LN_EPS = 1e-5
RMS_EPS = 1e-6
N_BRANCHES = 3
MASK_VALUE = -1e30

WA_Q = GQA_Q_HEADS * HEAD_DIM
WA_KV = GQA_KV_HEADS * HEAD_DIM
WB = NA_HEADS * HEAD_DIM
WM = MEM_HEADS * HEAD_DIM
COL_KA = WA_Q
COL_VA = COL_KA + WA_KV
COL_QB = COL_VA + WA_KV
COL_KB = COL_QB + WB
COL_VB = COL_KB + WB
COL_QM = COL_VB + WB
W_ATTN_COLS = COL_QM + WM

V7X_VMEM_BYTES = 64 * 1024 * 1024
VMEM_LIMIT_BYTES = V7X_VMEM_BYTES - 8 * 1024 * 1024

BF16 = jnp.bfloat16
F32 = jnp.float32


def _params(*semantics):
    return pltpu.CompilerParams(dimension_semantics=semantics,
                                vmem_limit_bytes=VMEM_LIMIT_BYTES)


def _dot(a, b):
    return jnp.dot(a, b, preferred_element_type=F32)


def _dot_nt(a, b):
    return lax.dot_general(a, b, (((1,), (1,)), ((), ())), preferred_element_type=F32)


def _layer_norm_rows(y, g, b):
    mu = jnp.mean(y, axis=-1, keepdims=True)
    yc = y - mu
    var = jnp.mean(yc * yc, axis=-1, keepdims=True)
    return yc * lax.rsqrt(var + LN_EPS) * g + b


LN_ROW_CHUNK = 128


def _for_row_chunks(n_rows, body):
    def step(r, carry):
        body(pl.ds(pl.multiple_of(r * LN_ROW_CHUNK, LN_ROW_CHUNK), LN_ROW_CHUNK))
        return carry
    lax.fori_loop(0, n_rows // LN_ROW_CHUNK, step, 0)


def _ffn_ln_kernel(x_ref, wg_ref, wu_ref, wd_ref, g_ref, b_ref, o_ref, xb_ref, *, alpha):
    j = pl.program_id(1)
    tm = x_ref.shape[0]

    @pl.when(j == 0)
    def _init():
        def body(rows):
            x = x_ref[rows, :]
            xb_ref[rows, :] = x.astype(BF16)
            o_ref[rows, :] = alpha * x
        _for_row_chunks(tm, body)

    xb = xb_ref[...]
    gate = _dot(xb, wg_ref[...])
    up = _dot(xb, wu_ref[...])
    h = (gate * jax.nn.sigmoid(gate) * (0.5 * up)).astype(BF16)
    o_ref[...] += _dot(h, wd_ref[...])

    @pl.when(j == pl.num_programs(1) - 1)
    def _finish():
        g = g_ref[...]
        b = b_ref[...]

        def body(rows):
            o_ref[rows, :] = _layer_norm_rows(o_ref[rows, :], g, b)
        _for_row_chunks(tm, body)


def _ffn_ln(x, w_gu, w_down, ln_g, ln_b, *, alpha, tm=1024, tf=512):
    t, d = x.shape
    f = w_down.shape[0]
    assert t % tm == 0 and f % tf == 0
    nf = f // tf
    return pl.pallas_call(
        functools.partial(_ffn_ln_kernel, alpha=alpha),
        grid=(t // tm, nf),
        in_specs=[
            pl.BlockSpec((tm, d), lambda i, j: (i, 0)),
            pl.BlockSpec((d, tf), lambda i, j: (0, j)),
            pl.BlockSpec((d, tf), lambda i, j: (0, j + nf)),
            pl.BlockSpec((tf, d), lambda i, j: (j, 0)),
            pl.BlockSpec((1, d), lambda i, j: (0, 0)),
            pl.BlockSpec((1, d), lambda i, j: (0, 0)),
        ],
        out_specs=pl.BlockSpec((tm, d), lambda i, j: (i, 0)),
        out_shape=jax.ShapeDtypeStruct((t, d), F32),
        scratch_shapes=[pltpu.VMEM((tm, d), BF16)],
        compiler_params=_params("parallel", "arbitrary"),
        name="ffn_ln",
    )(x, w_gu, w_gu, w_down, ln_g.reshape(1, d), ln_b.reshape(1, d))


def _rope_tables(seq):
    t = np.arange(seq)
    pos = np.stack([t // GRID_W, t % GRID_W], axis=1).astype(np.float32)
    lane = np.arange(HEAD_DIM)
    half = lane // (HEAD_DIM // 2)
    within = lane % (HEAD_DIM // 2)
    quarter = HEAD_DIM // 4
    first = within < quarter
    freq = within % quarter
    axis_rot = HEAD_DIM // 2
    inv = jnp.asarray(ROPE_THETA, F32) ** (
        -jnp.arange(0, axis_rot, 2, dtype=F32) / axis_rot)
    ang = jnp.asarray(pos)[:, half] * inv[freq][None, :]
    cos = jnp.cos(ang)
    sin = jnp.sin(ang)
    first = jnp.asarray(first)[None, :]
    sin_up = jnp.where(first, -sin, 0.0)
    sin_dn = jnp.where(first, 0.0, sin)
    return cos, sin_up, sin_dn


def _inproj_kernel(x_ref, w_ref, qg_ref, kg_ref, cos_ref, sup_ref, sdn_ref, o_ref):
    xb = x_ref[...].astype(BF16)
    cos = cos_ref[...]
    sup = sup_ref[...]
    sdn = sdn_ref[...]
    scale = HEAD_DIM ** -0.5
    quarter = HEAD_DIM // 4
    pair = 2 * HEAD_DIM

    def norm_rope(z, gain):
        z = z * lax.rsqrt(jnp.mean(z * z, axis=-1, keepdims=True) + RMS_EPS) * gain
        return (z * cos
                + pltpu.roll(z, HEAD_DIM - quarter, 1) * sup
                + pltpu.roll(z, quarter, 1) * sdn)

    for c in range(0, W_ATTN_COLS, pair):
        z2 = _dot(xb, w_ref[:, c:c + pair])
        for hh in range(2):
            col = c + hh * HEAD_DIM
            z = z2[:, hh * HEAD_DIM:(hh + 1) * HEAD_DIM]
            if col < COL_KA:
                z = norm_rope(z, qg_ref[...]) * scale
            elif col < COL_VA:
                z = norm_rope(z, kg_ref[...])
            elif COL_QB <= col < COL_KB or col >= COL_QM:
                z = z * scale
            o_ref[:, col:col + HEAD_DIM] = z.astype(BF16)


def _inproj(x, w_in, q_gain, k_gain, seq, *, tm=512):
    t, d = x.shape
    assert t % tm == 0 and seq % tm == 0
    cos, sup, sdn = _rope_tables(seq)
    nseq = seq // tm
    tab_spec = pl.BlockSpec((tm, HEAD_DIM), lambda i: (i % nseq, 0))
    gain_spec = pl.BlockSpec((1, HEAD_DIM), lambda i: (0, 0))
    return pl.pallas_call(
        _inproj_kernel,
        grid=(t // tm,),
        in_specs=[
            pl.BlockSpec((tm, d), lambda i: (i, 0)),
            pl.BlockSpec((d, W_ATTN_COLS), lambda i: (0, 0), pipeline_mode=pl.Buffered(1)),
            gain_spec, gain_spec, tab_spec, tab_spec, tab_spec,
        ],
        out_specs=pl.BlockSpec((tm, W_ATTN_COLS), lambda i: (i, 0)),
        out_shape=jax.ShapeDtypeStruct((t, W_ATTN_COLS), BF16),
        compiler_params=_params("parallel"),
        name="inproj",
    )(x, w_in, q_gain.reshape(1, HEAD_DIM), k_gain.reshape(1, HEAD_DIM), cos, sup, sdn)


def _memkv_kernel(m_ref, w_ref, o_ref):
    o_ref[...] = _dot(m_ref[...].astype(BF16), w_ref[...]).astype(BF16)


def _memkv(mem2d, w, *, tm=512):
    t, d = mem2d.shape
    n = w.shape[1]
    assert t % tm == 0
    return pl.pallas_call(
        _memkv_kernel,
        grid=(t // tm,),
        in_specs=[pl.BlockSpec((tm, d), lambda i: (i, 0)),
                  pl.BlockSpec((d, n), lambda i: (0, 0))],
        out_specs=pl.BlockSpec((tm, n), lambda i: (i, 0)),
        out_shape=jax.ShapeDtypeStruct((t, n), BF16),
        compiler_params=_params("parallel"),
        name="memkv",
    )(mem2d, w)


def _with_ones_column(v):
    lane = lax.broadcasted_iota(jnp.int32, v.shape, 1)
    ones_col = jnp.where(lane == 0, 1.0, 0.0).astype(v.dtype)
    return jnp.concatenate([v, ones_col], axis=1)


def _softmax_pv(s, v_ext):
    m = jnp.max(s, axis=-1, keepdims=True)
    p = jnp.exp(s - m).astype(BF16)
    ov = _dot(p, v_ext)
    return ov[:, :HEAD_DIM] / ov[:, HEAD_DIM:HEAD_DIM + 1]


def _softmax_pv_few_keys(s, v):
    m = jnp.max(s, axis=-1, keepdims=True)
    p = jnp.exp(s - m)
    l = jnp.sum(p, axis=-1, keepdims=True)
    return _dot(p.astype(BF16), v) / l


GQA_GROUP = GQA_Q_HEADS // GQA_KV_HEADS
GQA_SUB_ROWS = 256


def _gqa_kernel(q_ref, k_ref, v_ref, o_ref):
    v_ext = _with_ones_column(v_ref[...])
    for r in range(0, q_ref.shape[0], GQA_SUB_ROWS):
        rows = slice(r, r + GQA_SUB_ROWS)
        for h in range(GQA_GROUP):
            cols = slice(h * HEAD_DIM, (h + 1) * HEAD_DIM)
            s = _dot_nt(q_ref[rows, cols], k_ref[...])
            o_ref[rows, cols] = _softmax_pv(s, v_ext).astype(BF16)


def _gqa(z, batch, seq, *, tq=1024):
    t = z.shape[0]
    nq = seq // tq
    gw = GQA_GROUP * HEAD_DIM
    kcol = COL_KA // HEAD_DIM
    vcol = COL_VA // HEAD_DIM
    return pl.pallas_call(
        _gqa_kernel,
        grid=(batch, GQA_KV_HEADS, nq),
        in_specs=[
            pl.BlockSpec((tq, gw), lambda b, g, i: (b * nq + i, g)),
            pl.BlockSpec((seq, HEAD_DIM), lambda b, g, i: (b, kcol + g)),
            pl.BlockSpec((seq, HEAD_DIM), lambda b, g, i: (b, vcol + g)),
        ],
        out_specs=pl.BlockSpec((tq, gw), lambda b, g, i: (b * nq + i, g)),
        out_shape=jax.ShapeDtypeStruct((t, WA_Q), BF16),
        compiler_params=_params("parallel", "parallel", "parallel"),
        name="gqa",
    )(z, z, z)


NA_Q_ROWS = 4
NA_K_ROWS = NA_Q_ROWS + NA_ROWS_MAX
NA_TQ = NA_Q_ROWS * GRID_W
NA_TK = NA_K_ROWS * GRID_W


def _na_key_row_start(qblock, rows):
    return np.clip(qblock * NA_Q_ROWS - NA_ROWS_MAX // 2, 0, rows - NA_K_ROWS)


def _na_bias_tables(rpb, rows):
    heads, n_dr, n_dc = rpb.shape
    nblocks = rows // NA_Q_ROWS
    kr_win = min(NA_ROWS_MAX, rows)
    period = 2 * GRID_W
    assert period >= GRID_W + NA_COLS and n_dc == 2 * NA_COLS - 1
    v = rpb.astype(F32)
    vpad = jnp.concatenate(
        [v[..., NA_COLS - 1:], jnp.zeros((heads, n_dr, period - n_dc), F32), v[..., :NA_COLS - 1]],
        axis=-1)
    skew = jnp.tile(vpad, (1, 1, GRID_W))[..., :GRID_W * (period - 1)]
    toep = skew.reshape(heads, n_dr, GRID_W, period - 1)[..., :GRID_W]
    qc = np.arange(GRID_W)
    cs = np.clip(qc - NA_COLS // 2, 0, GRID_W - NA_COLS)
    ok_c = (qc[None, :] >= cs[:, None]) & (qc[None, :] < cs[:, None] + NA_COLS)
    toep = jnp.where(jnp.asarray(ok_c)[None, None], toep, MASK_VALUE)
    masked = jnp.full((heads, GRID_W, GRID_W), MASK_VALUE, F32)
    tabs = []
    for qblock in (0, 1, nblocks - 1):
        ks = _na_key_row_start(qblock, rows)
        q_rows = []
        for qi in range(NA_Q_ROWS):
            qr = qblock * NA_Q_ROWS + qi
            rs = np.clip(qr - kr_win // 2, 0, rows - kr_win)
            blocks = []
            for kj in range(NA_K_ROWS):
                kr = ks + kj
                inside = rs <= kr < rs + kr_win
                blocks.append(toep[:, kr - qr + NA_ROWS_MAX - 1] if inside else masked)
            q_rows.append(jnp.concatenate(blocks, axis=-1))
        tabs.append(jnp.concatenate(q_rows, axis=-2))
    return jnp.stack(tabs)


def _na_kernel(q_ref, k_ref, v_ref, bias_ref, o_ref, *, n_blocks):
    i = pl.program_id(1)
    kblock = jnp.clip(i - NA_ROWS_MAX // 2 // NA_Q_ROWS, 0, n_blocks - NA_K_ROWS // NA_Q_ROWS)
    start = pl.multiple_of(kblock * NA_TQ, NA_TQ)
    for h in range(NA_HEADS):
        cols = slice(h * HEAD_DIM, (h + 1) * HEAD_DIM)
        k = k_ref[pl.ds(start, NA_TK), cols]
        v = v_ref[pl.ds(start, NA_TK), cols]
        s = _dot_nt(q_ref[:, cols], k) + bias_ref[0, h]
        o_ref[:, cols] = _softmax_pv(s, _with_ones_column(v)).astype(BF16)


def _na(z, rpb, batch, seq):
    t = z.shape[0]
    rows = seq // GRID_W
    assert rows % NA_Q_ROWS == 0 and rows >= NA_K_ROWS and NA_ROWS_MAX % (2 * NA_Q_ROWS) == 0
    n_blocks = rows // NA_Q_ROWS
    bias = _na_bias_tables(rpb, rows)

    def pattern(b, i):
        return jnp.where(i == 0, 0, jnp.where(i == n_blocks - 1, 2, 1))

    return pl.pallas_call(
        functools.partial(_na_kernel, n_blocks=n_blocks),
        grid=(batch, n_blocks),
        in_specs=[
            pl.BlockSpec((NA_TQ, WB), lambda b, i: (b * n_blocks + i, COL_QB // WB)),
            pl.BlockSpec((seq, WB), lambda b, i: (b, COL_KB // WB)),
            pl.BlockSpec((seq, WB), lambda b, i: (b, COL_VB // WB)),
            pl.BlockSpec((1, NA_HEADS, NA_TQ, NA_TK), lambda b, i: (pattern(b, i), 0, 0, 0)),
        ],
        out_specs=pl.BlockSpec((NA_TQ, WB), lambda b, i: (b * n_blocks + i, 0)),
        out_shape=jax.ShapeDtypeStruct((t, WB), BF16),
        compiler_params=_params("parallel", "arbitrary"),
        name="natten",
    )(z, z, z, bias)


def _mem_attn_kernel(q_ref, kv_ref, o_ref):
    for h in range(MEM_HEADS):
        cols = slice(h * HEAD_DIM, (h + 1) * HEAD_DIM)
        s = _dot_nt(q_ref[:, cols], kv_ref[:, cols])
        v = kv_ref[:, WM + h * HEAD_DIM:WM + (h + 1) * HEAD_DIM]
        o_ref[:, cols] = _softmax_pv_few_keys(s, v).astype(BF16)


def _mem_attn(z, kvm, batch, seq, mem_len, *, tq=512):
    t = z.shape[0]
    nq = seq // tq
    return pl.pallas_call(
        _mem_attn_kernel,
        grid=(batch, nq),
        in_specs=[
            pl.BlockSpec((tq, WM), lambda b, i: (b * nq + i, COL_QM // WM)),
            pl.BlockSpec((mem_len, 2 * WM), lambda b, i: (b, 0)),
        ],
        out_specs=pl.BlockSpec((tq, WM), lambda b, i: (b * nq + i, 0)),
        out_shape=jax.ShapeDtypeStruct((t, WM), BF16),
        compiler_params=_params("parallel", "parallel"),
        name="mem_attn",
    )(z, kvm)


def _merge_ln_kernel(x_ref, oa_ref, ob_ref, om_ref, wg0_ref, wg1_ref, wg2_ref,
                     bg_ref, woa_ref, wob_ref, wom_ref, wout_ref, g_ref, b_ref, o_ref,
                     xb_ref, *, alpha):
    c = pl.program_id(1)
    tm = x_ref.shape[0]

    @pl.when(c == 0)
    def _init():
        def body(rows):
            x = x_ref[rows, :]
            xb_ref[rows, :] = x.astype(BF16)
            o_ref[rows, :] = alpha * x
        _for_row_chunks(tm, body)

    xb = xb_ref[...]
    bg = bg_ref[...]

    def gated(k, wg_ref, br_ref, wo_ref):
        gate = jax.nn.sigmoid(_dot(xb, wg_ref[...]) + bg[k:k + 1, :])
        return gate * _dot(br_ref[...], wo_ref[...])

    y = (gated(0, wg0_ref, oa_ref, woa_ref)
         + gated(1, wg1_ref, ob_ref, wob_ref)
         + gated(2, wg2_ref, om_ref, wom_ref))
    o_ref[...] += _dot(y.astype(BF16), wout_ref[...])

    @pl.when(c == pl.num_programs(1) - 1)
    def _finish():
        g = g_ref[...]
        b = b_ref[...]

        def body(rows):
            o_ref[rows, :] = _layer_norm_rows(o_ref[rows, :], g, b)
        _for_row_chunks(tm, body)


def _merge_ln(x, oa, ob, om, w_in, b_gate, w_oa, w_ob, w_om, w_out, ln_g, ln_b,
              *, alpha, tm=1024, tn=256):
    t, d = x.shape
    assert t % tm == 0 and d % tn == 0 and W_ATTN_COLS % tn == 0
    nc = d // tn
    gate0 = W_ATTN_COLS // tn

    def rows(width):
        return pl.BlockSpec((tm, width), lambda i, c: (i, 0))

    def gate_w(k):
        return pl.BlockSpec((d, tn), lambda i, c: (0, gate0 + k * nc + c))

    def branch_w(width):
        return pl.BlockSpec((width, tn), lambda i, c: (0, c))

    vec = pl.BlockSpec((1, d), lambda i, c: (0, 0))
    return pl.pallas_call(
        functools.partial(_merge_ln_kernel, alpha=alpha),
        grid=(t // tm, nc),
        in_specs=[
            pl.BlockSpec((tm, d), lambda i, c: (i, 0), pipeline_mode=pl.Buffered(1)),
            rows(WA_Q), rows(WB), rows(WM),
            gate_w(0), gate_w(1), gate_w(2),
            pl.BlockSpec((N_BRANCHES, tn), lambda i, c: (0, c)),
            branch_w(WA_Q), branch_w(WB), branch_w(WM),
            pl.BlockSpec((tn, d), lambda i, c: (c, 0)),
            vec, vec,
        ],
        out_specs=rows(d),
        out_shape=jax.ShapeDtypeStruct((t, d), F32),
        scratch_shapes=[pltpu.VMEM((tm, d), BF16)],
        compiler_params=_params("parallel", "arbitrary"),
        name="merge_ln",
    )(x, oa, ob, om, w_in, w_in, w_in, b_gate.reshape(N_BRANCHES, d),
      w_oa, w_ob, w_om, w_out, ln_g.reshape(1, d), ln_b.reshape(1, d))


def kernel(x, mem, ln1_g, ln1_b, ffn1_w_gu, ffn1_w_down, w_in, b_gate, q_norm_a, k_norm_a,
           na_rpb, w_mem_kv, w_oa, w_ob, w_om, w_out, ln2_g, ln2_b, ffn2_w_gu, ffn2_w_down,
           ln3_g, ln3_b):
    batch, seq, d = x.shape
    mem_len = mem.shape[1]
    depth = w_in.shape[0]
    alpha = (2 * depth) ** 0.25
    h = x.reshape(batch * seq, d)
    mem2d = mem.reshape(batch * mem_len, d)
    for l in range(depth):
        bf = lambda w: w[l].astype(BF16)
        h = _ffn_ln(h, bf(ffn1_w_gu), bf(ffn1_w_down), ln1_g[l], ln1_b[l], alpha=alpha)
        w_in_b = bf(w_in)
        z = _inproj(h, w_in_b, q_norm_a[l], k_norm_a[l], seq)
        kvm = _memkv(mem2d, bf(w_mem_kv))
        oa = _gqa(z, batch, seq)
        ob = _na(z, na_rpb[l], batch, seq)
        om = _mem_attn(z, kvm, batch, seq, mem_len)
        h = _merge_ln(h, oa, ob, om, w_in_b, b_gate[l], bf(w_oa), bf(w_ob), bf(w_om),
                      bf(w_out), ln2_g[l], ln2_b[l], alpha=alpha)
        h = _ffn_ln(h, bf(ffn2_w_gu), bf(ffn2_w_down), ln3_g[l], ln3_b[l], alpha=alpha)
    return h.reshape(batch, seq, d)
```

```python
import functools

import numpy as np
import jax
import jax.numpy as jnp
from jax import lax
from jax.experimental import pallas as pl
from jax.experimental.pallas import tpu as pltpu

HEAD_DIM = 128
GQA_Q_HEADS = 8
GQA_KV_HEADS = 2
NA_HEADS = 4
MEM_HEADS = 4
GRID_W = 64
NA_ROWS_MAX = 8
NA_COLS = 16
ROPE_THETA = 10000.0
LN_EPS = 1e-5
RMS_EPS = 1e-6
N_BRANCHES = 3
MASK_VALUE = -1e30

WA_Q = GQA_Q_HEADS * HEAD_DIM
WA_KV = GQA_KV_HEADS * HEAD_DIM
WB = NA_HEADS * HEAD_DIM
WM = MEM_HEADS * HEAD_DIM
COL_KA = WA_Q
COL_VA = COL_KA + WA_KV
COL_QB = COL_VA + WA_KV
COL_KB = COL_QB + WB
COL_VB = COL_KB + WB
COL_QM = COL_VB + WB
W_ATTN_COLS = COL_QM + WM

V7X_VMEM_BYTES = 64 * 1024 * 1024
VMEM_LIMIT_BYTES = V7X_VMEM_BYTES - 8 * 1024 * 1024

BF16 = jnp.bfloat16
F32 = jnp.float32


def _params(*semantics):
    return pltpu.CompilerParams(dimension_semantics=semantics,
                                vmem_limit_bytes=VMEM_LIMIT_BYTES)


def _dot(a, b):
    return jnp.dot(a, b, preferred_element_type=F32)


def _dot_nt(a, b):
    return lax.dot_general(a, b, (((1,), (1,)), ((), ())), preferred_element_type=F32)


def _layer_norm_rows(y, g, b):
    mu = jnp.mean(y, axis=-1, keepdims=True)
    yc = y - mu
    var = jnp.mean(yc * yc, axis=-1, keepdims=True)
    return yc * lax.rsqrt(var + LN_EPS) * g + b


LN_ROW_CHUNK = 128


def _for_row_chunks(n_rows, body):
    def step(r, carry):
        body(pl.ds(pl.multiple_of(r * LN_ROW_CHUNK, LN_ROW_CHUNK), LN_ROW_CHUNK))
        return carry
    lax.fori_loop(0, n_rows // LN_ROW_CHUNK, step, 0)


def _ffn_ln_kernel(x_ref, wg_ref, wu_ref, wd_ref, g_ref, b_ref, o_ref, xb_ref, *, alpha):
    j = pl.program_id(1)
    tm = x_ref.shape[0]

    @pl.when(j == 0)
    def _init():
        def body(rows):
            x = x_ref[rows, :]
            xb_ref[rows, :] = x.astype(BF16)
            o_ref[rows, :] = alpha * x
        _for_row_chunks(tm, body)

    xb = xb_ref[...]
    gate = _dot(xb, wg_ref[...])
    up = _dot(xb, wu_ref[...])
    h = (gate * jax.nn.sigmoid(gate) * (0.5 * up)).astype(BF16)
    o_ref[...] += _dot(h, wd_ref[...])

    @pl.when(j == pl.num_programs(1) - 1)
    def _finish():
        g = g_ref[...]
        b = b_ref[...]

        def body(rows):
            o_ref[rows, :] = _layer_norm_rows(o_ref[rows, :], g, b)
        _for_row_chunks(tm, body)


def _ffn_ln(x, w_gu, w_down, ln_g, ln_b, *, alpha, tm=1024, tf=512):
    t, d = x.shape
    f = w_down.shape[0]
    assert t % tm == 0 and f % tf == 0
    nf = f // tf
    return pl.pallas_call(
        functools.partial(_ffn_ln_kernel, alpha=alpha),
        grid=(t // tm, nf),
        in_specs=[
            pl.BlockSpec((tm, d), lambda i, j: (i, 0)),
            pl.BlockSpec((d, tf), lambda i, j: (0, j)),
            pl.BlockSpec((d, tf), lambda i, j: (0, j + nf)),
            pl.BlockSpec((tf, d), lambda i, j: (j, 0)),
            pl.BlockSpec((1, d), lambda i, j: (0, 0)),
            pl.BlockSpec((1, d), lambda i, j: (0, 0)),
        ],
        out_specs=pl.BlockSpec((tm, d), lambda i, j: (i, 0)),
        out_shape=jax.ShapeDtypeStruct((t, d), F32),
        scratch_shapes=[pltpu.VMEM((tm, d), BF16)],
        compiler_params=_params("parallel", "arbitrary"),
        name="ffn_ln",
    )(x, w_gu, w_gu, w_down, ln_g.reshape(1, d), ln_b.reshape(1, d))


def _rope_tables(seq):
    t = np.arange(seq)
    pos = np.stack([t // GRID_W, t % GRID_W], axis=1).astype(np.float32)
    lane = np.arange(HEAD_DIM)
    half = lane // (HEAD_DIM // 2)
    within = lane % (HEAD_DIM // 2)
    quarter = HEAD_DIM // 4
    first = within < quarter
    freq = within % quarter
    axis_rot = HEAD_DIM // 2
    inv = jnp.asarray(ROPE_THETA, F32) ** (
        -jnp.arange(0, axis_rot, 2, dtype=F32) / axis_rot)
    ang = jnp.asarray(pos)[:, half] * inv[freq][None, :]
    cos = jnp.cos(ang)
    sin = jnp.sin(ang)
    first = jnp.asarray(first)[None, :]
    sin_up = jnp.where(first, -sin, 0.0)
    sin_dn = jnp.where(first, 0.0, sin)
    return cos, sin_up, sin_dn


def _inproj_kernel(x_ref, w_ref, qg_ref, kg_ref, cos_ref, sup_ref, sdn_ref, o_ref):
    xb = x_ref[...].astype(BF16)
    cos = cos_ref[...]
    sup = sup_ref[...]
    sdn = sdn_ref[...]
    scale = HEAD_DIM ** -0.5
    quarter = HEAD_DIM // 4
    pair = 2 * HEAD_DIM

    def norm_rope(z, gain):
        z = z * lax.rsqrt(jnp.mean(z * z, axis=-1, keepdims=True) + RMS_EPS) * gain
        return (z * cos
                + pltpu.roll(z, HEAD_DIM - quarter, 1) * sup
                + pltpu.roll(z, quarter, 1) * sdn)

    for c in range(0, W_ATTN_COLS, pair):
        z2 = _dot(xb, w_ref[:, c:c + pair])
        for hh in range(2):
            col = c + hh * HEAD_DIM
            z = z2[:, hh * HEAD_DIM:(hh + 1) * HEAD_DIM]
            if col < COL_KA:
                z = norm_rope(z, qg_ref[...]) * scale
            elif col < COL_VA:
                z = norm_rope(z, kg_ref[...])
            elif COL_QB <= col < COL_KB or col >= COL_QM:
                z = z * scale
            o_ref[:, col:col + HEAD_DIM] = z.astype(BF16)


def _inproj(x, w_in, q_gain, k_gain, seq, *, tm=512):
    t, d = x.shape
    assert t % tm == 0 and seq % tm == 0
    cos, sup, sdn = _rope_tables(seq)
    nseq = seq // tm
    tab_spec = pl.BlockSpec((tm, HEAD_DIM), lambda i: (i % nseq, 0))
    gain_spec = pl.BlockSpec((1, HEAD_DIM), lambda i: (0, 0))
    return pl.pallas_call(
        _inproj_kernel,
        grid=(t // tm,),
        in_specs=[
            pl.BlockSpec((tm, d), lambda i: (i, 0)),
            pl.BlockSpec((d, W_ATTN_COLS), lambda i: (0, 0), pipeline_mode=pl.Buffered(1)),
            gain_spec, gain_spec, tab_spec, tab_spec, tab_spec,
        ],
        out_specs=pl.BlockSpec((tm, W_ATTN_COLS), lambda i: (i, 0)),
        out_shape=jax.ShapeDtypeStruct((t, W_ATTN_COLS), BF16),
        compiler_params=_params("parallel"),
        name="inproj",
    )(x, w_in, q_gain.reshape(1, HEAD_DIM), k_gain.reshape(1, HEAD_DIM), cos, sup, sdn)


def _memkv_kernel(m_ref, w_ref, o_ref):
    o_ref[...] = _dot(m_ref[...].astype(BF16), w_ref[...]).astype(BF16)


def _memkv(mem2d, w, *, tm=512):
    t, d = mem2d.shape
    n = w.shape[1]
    assert t % tm == 0
    return pl.pallas_call(
        _memkv_kernel,
        grid=(t // tm,),
        in_specs=[pl.BlockSpec((tm, d), lambda i: (i, 0)),
                  pl.BlockSpec((d, n), lambda i: (0, 0))],
        out_specs=pl.BlockSpec((tm, n), lambda i: (i, 0)),
        out_shape=jax.ShapeDtypeStruct((t, n), BF16),
        compiler_params=_params("parallel"),
        name="memkv",
    )(mem2d, w)


def _with_ones_column(v):
    lane = lax.broadcasted_iota(jnp.int32, v.shape, 1)
    ones_col = jnp.where(lane == 0, 1.0, 0.0).astype(v.dtype)
    return jnp.concatenate([v, ones_col], axis=1)


def _softmax_pv(s, v_ext):
    m = jnp.max(s, axis=-1, keepdims=True)
    p = jnp.exp(s - m).astype(BF16)
    ov = _dot(p, v_ext)
    return ov[:, :HEAD_DIM] / ov[:, HEAD_DIM:HEAD_DIM + 1]


def _softmax_pv_few_keys(s, v):
    m = jnp.max(s, axis=-1, keepdims=True)
    p = jnp.exp(s - m)
    l = jnp.sum(p, axis=-1, keepdims=True)
    return _dot(p.astype(BF16), v) / l


GQA_GROUP = GQA_Q_HEADS // GQA_KV_HEADS
GQA_SUB_ROWS = 256


def _gqa_kernel(q_ref, k_ref, v_ref, o_ref):
    v_ext = _with_ones_column(v_ref[...])
    for r in range(0, q_ref.shape[0], GQA_SUB_ROWS):
        rows = slice(r, r + GQA_SUB_ROWS)
        for h in range(GQA_GROUP):
            cols = slice(h * HEAD_DIM, (h + 1) * HEAD_DIM)
            s = _dot_nt(q_ref[rows, cols], k_ref[...])
            o_ref[rows, cols] = _softmax_pv(s, v_ext).astype(BF16)


def _gqa(z, batch, seq, *, tq=1024):
    t = z.shape[0]
    nq = seq // tq
    gw = GQA_GROUP * HEAD_DIM
    kcol = COL_KA // HEAD_DIM
    vcol = COL_VA // HEAD_DIM
    return pl.pallas_call(
        _gqa_kernel,
        grid=(batch, GQA_KV_HEADS, nq),
        in_specs=[
            pl.BlockSpec((tq, gw), lambda b, g, i: (b * nq + i, g)),
            pl.BlockSpec((seq, HEAD_DIM), lambda b, g, i: (b, kcol + g)),
            pl.BlockSpec((seq, HEAD_DIM), lambda b, g, i: (b, vcol + g)),
        ],
        out_specs=pl.BlockSpec((tq, gw), lambda b, g, i: (b * nq + i, g)),
        out_shape=jax.ShapeDtypeStruct((t, WA_Q), BF16),
        compiler_params=_params("parallel", "parallel", "parallel"),
        name="gqa",
    )(z, z, z)


NA_Q_ROWS = 4
NA_K_ROWS = NA_Q_ROWS + NA_ROWS_MAX
NA_TQ = NA_Q_ROWS * GRID_W
NA_TK = NA_K_ROWS * GRID_W


def _na_key_row_start(qblock, rows):
    return np.clip(qblock * NA_Q_ROWS - NA_ROWS_MAX // 2, 0, rows - NA_K_ROWS)


def _na_bias_tables(rpb, rows):
    heads, n_dr, n_dc = rpb.shape
    nblocks = rows // NA_Q_ROWS
    kr_win = min(NA_ROWS_MAX, rows)
    period = 2 * GRID_W
    assert period >= GRID_W + NA_COLS and n_dc == 2 * NA_COLS - 1
    v = rpb.astype(F32)
    vpad = jnp.concatenate(
        [v[..., NA_COLS - 1:], jnp.zeros((heads, n_dr, period - n_dc), F32), v[..., :NA_COLS - 1]],
        axis=-1)
    skew = jnp.tile(vpad, (1, 1, GRID_W))[..., :GRID_W * (period - 1)]
    toep = skew.reshape(heads, n_dr, GRID_W, period - 1)[..., :GRID_W]
    qc = np.arange(GRID_W)
    cs = np.clip(qc - NA_COLS // 2, 0, GRID_W - NA_COLS)
    ok_c = (qc[None, :] >= cs[:, None]) & (qc[None, :] < cs[:, None] + NA_COLS)
    toep = jnp.where(jnp.asarray(ok_c)[None, None], toep, MASK_VALUE)
    masked = jnp.full((heads, GRID_W, GRID_W), MASK_VALUE, F32)
    tabs = []
    for qblock in (0, 1, nblocks - 1):
        ks = _na_key_row_start(qblock, rows)
        q_rows = []
        for qi in range(NA_Q_ROWS):
            qr = qblock * NA_Q_ROWS + qi
            rs = np.clip(qr - kr_win // 2, 0, rows - kr_win)
            blocks = []
            for kj in range(NA_K_ROWS):
                kr = ks + kj
                inside = rs <= kr < rs + kr_win
                blocks.append(toep[:, kr - qr + NA_ROWS_MAX - 1] if inside else masked)
            q_rows.append(jnp.concatenate(blocks, axis=-1))
        tabs.append(jnp.concatenate(q_rows, axis=-2))
    return jnp.stack(tabs)


def _na_kernel(q_ref, k_ref, v_ref, bias_ref, o_ref, *, n_blocks):
    i = pl.program_id(1)
    kblock = jnp.clip(i - NA_ROWS_MAX // 2 // NA_Q_ROWS, 0, n_blocks - NA_K_ROWS // NA_Q_ROWS)
    start = pl.multiple_of(kblock * NA_TQ, NA_TQ)
    for h in range(NA_HEADS):
        cols = slice(h * HEAD_DIM, (h + 1) * HEAD_DIM)
        k = k_ref[pl.ds(start, NA_TK), cols]
        v = v_ref[pl.ds(start, NA_TK), cols]
        s = _dot_nt(q_ref[:, cols], k) + bias_ref[0, h]
        o_ref[:, cols] = _softmax_pv(s, _with_ones_column(v)).astype(BF16)


def _na(z, rpb, batch, seq):
    t = z.shape[0]
    rows = seq // GRID_W
    assert rows % NA_Q_ROWS == 0 and rows >= NA_K_ROWS and NA_ROWS_MAX % (2 * NA_Q_ROWS) == 0
    n_blocks = rows // NA_Q_ROWS
    bias = _na_bias_tables(rpb, rows)

    def pattern(b, i):
        return jnp.where(i == 0, 0, jnp.where(i == n_blocks - 1, 2, 1))

    return pl.pallas_call(
        functools.partial(_na_kernel, n_blocks=n_blocks),
        grid=(batch, n_blocks),
        in_specs=[
            pl.BlockSpec((NA_TQ, WB), lambda b, i: (b * n_blocks + i, COL_QB // WB)),
            pl.BlockSpec((seq, WB), lambda b, i: (b, COL_KB // WB)),
            pl.BlockSpec((seq, WB), lambda b, i: (b, COL_VB // WB)),
            pl.BlockSpec((1, NA_HEADS, NA_TQ, NA_TK), lambda b, i: (pattern(b, i), 0, 0, 0)),
        ],
        out_specs=pl.BlockSpec((NA_TQ, WB), lambda b, i: (b * n_blocks + i, 0)),
        out_shape=jax.ShapeDtypeStruct((t, WB), BF16),
        compiler_params=_params("parallel", "arbitrary"),
        name="natten",
    )(z, z, z, bias)


def _mem_attn_kernel(q_ref, kv_ref, o_ref):
    for h in range(MEM_HEADS):
        cols = slice(h * HEAD_DIM, (h + 1) * HEAD_DIM)
        s = _dot_nt(q_ref[:, cols], kv_ref[:, cols])
        v = kv_ref[:, WM + h * HEAD_DIM:WM + (h + 1) * HEAD_DIM]
        o_ref[:, cols] = _softmax_pv_few_keys(s, v).astype(BF16)


def _mem_attn(z, kvm, batch, seq, mem_len, *, tq=512):
    t = z.shape[0]
    nq = seq // tq
    return pl.pallas_call(
        _mem_attn_kernel,
        grid=(batch, nq),
        in_specs=[
            pl.BlockSpec((tq, WM), lambda b, i: (b * nq + i, COL_QM // WM)),
            pl.BlockSpec((mem_len, 2 * WM), lambda b, i: (b, 0)),
        ],
        out_specs=pl.BlockSpec((tq, WM), lambda b, i: (b * nq + i, 0)),
        out_shape=jax.ShapeDtypeStruct((t, WM), BF16),
        compiler_params=_params("parallel", "parallel"),
        name="mem_attn",
    )(z, kvm)


MXU_COLS = 256


def _gated_merge_kernel(x_ref, oa_ref, ob_ref, om_ref, wg_ref, bg_ref,
                        woa_ref, wob_ref, wom_ref, y_ref):
    d = x_ref.shape[1]
    xb = x_ref[...].astype(BF16)
    branches = ((oa_ref[...], woa_ref), (ob_ref[...], wob_ref), (om_ref[...], wom_ref))
    for c in range(0, d, MXU_COLS):
        cols = slice(c, c + MXU_COLS)
        y = None
        for k, (o, wo_ref) in enumerate(branches):
            gcols = slice(k * d + c, k * d + c + MXU_COLS)
            gate = jax.nn.sigmoid(_dot(xb, wg_ref[:, gcols]) + bg_ref[:, gcols])
            term = gate * _dot(o, wo_ref[:, cols])
            y = term if y is None else y + term
        y_ref[:, cols] = y.astype(BF16)


def _gated_merge(x, oa, ob, om, w_gate, b_gate, w_oa, w_ob, w_om, *, tm=512):
    t, d = x.shape
    assert t % tm == 0 and d % MXU_COLS == 0 and w_gate.shape == (d, N_BRANCHES * d)

    def rows(width):
        return pl.BlockSpec((tm, width), lambda i: (i, 0))

    def resident(arr):
        return pl.BlockSpec(arr.shape, lambda i: (0, 0), pipeline_mode=pl.Buffered(1))

    b_gate = b_gate.reshape(1, N_BRANCHES * d)
    return pl.pallas_call(
        _gated_merge_kernel,
        grid=(t // tm,),
        in_specs=[rows(d), rows(WA_Q), rows(WB), rows(WM),
                  resident(w_gate), resident(b_gate),
                  resident(w_oa), resident(w_ob), resident(w_om)],
        out_specs=rows(d),
        out_shape=jax.ShapeDtypeStruct((t, d), BF16),
        compiler_params=_params("parallel"),
        name="gated_merge",
    )(x, oa, ob, om, w_gate, b_gate, w_oa, w_ob, w_om)


def _outproj_ln_kernel(x_ref, y_ref, w_ref, g_ref, b_ref, o_ref, *, alpha):
    tm = x_ref.shape[0]
    o_ref[...] = alpha * x_ref[...] + _dot(y_ref[...], w_ref[...])
    g = g_ref[...]
    b = b_ref[...]

    def body(rows):
        o_ref[rows, :] = _layer_norm_rows(o_ref[rows, :], g, b)
    _for_row_chunks(tm, body)


def _outproj_ln(x, y, w_out, ln_g, ln_b, *, alpha, tm=512):
    t, d = x.shape
    assert t % tm == 0
    vec = pl.BlockSpec((1, d), lambda i: (0, 0))
    return pl.pallas_call(
        functools.partial(_outproj_ln_kernel, alpha=alpha),
        grid=(t // tm,),
        in_specs=[
            pl.BlockSpec((tm, d), lambda i: (i, 0)),
            pl.BlockSpec((tm, d), lambda i: (i, 0)),
            pl.BlockSpec((d, d), lambda i: (0, 0), pipeline_mode=pl.Buffered(1)),
            vec, vec,
        ],
        out_specs=pl.BlockSpec((tm, d), lambda i: (i, 0)),
        out_shape=jax.ShapeDtypeStruct((t, d), F32),
        compiler_params=_params("parallel"),
        name="outproj_ln",
    )(x, y, w_out, ln_g.reshape(1, d), ln_b.reshape(1, d))


def kernel(x, mem, ln1_g, ln1_b, ffn1_w_gu, ffn1_w_down, w_in, b_gate, q_norm_a, k_norm_a,
           na_rpb, w_mem_kv, w_oa, w_ob, w_om, w_out, ln2_g, ln2_b, ffn2_w_gu, ffn2_w_down,
           ln3_g, ln3_b):
    batch, seq, d = x.shape
    mem_len = mem.shape[1]
    depth = w_in.shape[0]
    alpha = (2 * depth) ** 0.25
    h = x.reshape(batch * seq, d)
    mem2d = mem.reshape(batch * mem_len, d)
    for l in range(depth):
        bf = lambda w: w[l].astype(BF16)
        h = _ffn_ln(h, bf(ffn1_w_gu), bf(ffn1_w_down), ln1_g[l], ln1_b[l], alpha=alpha)
        w_attn = w_in[l][:, :W_ATTN_COLS].astype(BF16)
        w_gate = w_in[l][:, W_ATTN_COLS:].astype(BF16)
        z = _inproj(h, w_attn, q_norm_a[l], k_norm_a[l], seq)
        kvm = _memkv(mem2d, bf(w_mem_kv))
        oa = _gqa(z, batch, seq)
        ob = _na(z, na_rpb[l], batch, seq)
        om = _mem_attn(z, kvm, batch, seq, mem_len)
        y = _gated_merge(h, oa, ob, om, w_gate, b_gate[l], bf(w_oa), bf(w_ob), bf(w_om))
        h = _outproj_ln(h, y, bf(w_out), ln2_g[l], ln2_b[l], alpha=alpha)
        h = _ffn_ln(h, bf(ffn2_w_gu), bf(ffn2_w_down), ln3_g[l], ln3_b[l], alpha=alpha)
    return h.reshape(batch, seq, d)
```

```python
import functools

import numpy as np
import jax
import jax.numpy as jnp
from jax import lax
from jax.experimental import pallas as pl
from jax.experimental.pallas import tpu as pltpu

HEAD_DIM = 128
GQA_Q_HEADS = 8
GQA_KV_HEADS = 2
NA_HEADS = 4
MEM_HEADS = 4
GRID_W = 64
NA_ROWS_MAX = 8
NA_COLS = 16
ROPE_THETA = 10000.0
LN_EPS = 1e-5
RMS_EPS = 1e-6
N_BRANCHES = 3
MASK_VALUE = -1e30

WA_Q = GQA_Q_HEADS * HEAD_DIM
WA_KV = GQA_KV_HEADS * HEAD_DIM
WB = NA_HEADS * HEAD_DIM
WM = MEM_HEADS * HEAD_DIM
COL_KA = WA_Q
COL_VA = COL_KA + WA_KV
COL_QB = COL_VA + WA_KV
COL_KB = COL_QB + WB
COL_VB = COL_KB + WB
COL_QM = COL_VB + WB
W_ATTN_COLS = COL_QM + WM

V7X_VMEM_BYTES = 64 * 1024 * 1024
VMEM_LIMIT_BYTES = V7X_VMEM_BYTES - 8 * 1024 * 1024

BF16 = jnp.bfloat16
F32 = jnp.float32


def _params(*semantics):
    return pltpu.CompilerParams(dimension_semantics=semantics,
                                vmem_limit_bytes=VMEM_LIMIT_BYTES)


def _dot(a, b):
    return jnp.dot(a, b, preferred_element_type=F32)


def _dot_nt(a, b):
    return lax.dot_general(a, b, (((1,), (1,)), ((), ())), preferred_element_type=F32)


def _layer_norm_rows(y, g, b):
    mu = jnp.mean(y, axis=-1, keepdims=True)
    yc = y - mu
    var = jnp.mean(yc * yc, axis=-1, keepdims=True)
    return yc * lax.rsqrt(var + LN_EPS) * g + b


LN_ROW_CHUNK = 128


LN_OVERLAP_ROWS = 256


def _for_row_chunks(n_rows, body):
    def step(r, carry):
        body(pl.ds(pl.multiple_of(r * LN_ROW_CHUNK, LN_ROW_CHUNK), LN_ROW_CHUNK))
        return carry
    lax.fori_loop(0, n_rows // LN_ROW_CHUNK, step, 0)


def _ffn_ln_kernel(x_ref, wg_ref, wu_ref, wd_ref, g_ref, b_ref, o_ref, xb_ref, *, alpha):
    j = pl.program_id(1)
    tm = x_ref.shape[0]

    @pl.when(j == 0)
    def _init():
        def body(rows):
            x = x_ref[rows, :]
            xb_ref[rows, :] = x.astype(BF16)
            o_ref[rows, :] = alpha * x
        _for_row_chunks(tm, body)

    def hidden():
        xb = xb_ref[...]
        gate = _dot(xb, wg_ref[...])
        up = _dot(xb, wu_ref[...])
        return (gate * jax.nn.sigmoid(gate) * (0.5 * up)).astype(BF16)

    last = pl.num_programs(1) - 1

    @pl.when(j < last)
    def _accumulate():
        o_ref[...] += _dot(hidden(), wd_ref[...])

    @pl.when(j == last)
    def _finish():
        h = hidden()
        g = g_ref[...]
        b = b_ref[...]
        for r in range(0, tm, LN_OVERLAP_ROWS):
            rows = slice(r, r + LN_OVERLAP_ROWS)
            y = o_ref[rows, :] + _dot(h[rows, :], wd_ref[...])
            o_ref[rows, :] = _layer_norm_rows(y, g, b)


def _ffn_ln(x, w_gu, w_down, ln_g, ln_b, *, alpha, tm=1024, tf=512):
    t, d = x.shape
    f = w_down.shape[0]
    assert t % tm == 0 and f % tf == 0
    nf = f // tf
    return pl.pallas_call(
        functools.partial(_ffn_ln_kernel, alpha=alpha),
        grid=(t // tm, nf),
        in_specs=[
            pl.BlockSpec((tm, d), lambda i, j: (i, 0)),
            pl.BlockSpec((d, tf), lambda i, j: (0, j)),
            pl.BlockSpec((d, tf), lambda i, j: (0, j + nf)),
            pl.BlockSpec((tf, d), lambda i, j: (j, 0)),
            pl.BlockSpec((1, d), lambda i, j: (0, 0)),
            pl.BlockSpec((1, d), lambda i, j: (0, 0)),
        ],
        out_specs=pl.BlockSpec((tm, d), lambda i, j: (i, 0)),
        out_shape=jax.ShapeDtypeStruct((t, d), F32),
        scratch_shapes=[pltpu.VMEM((tm, d), BF16)],
        compiler_params=_params("parallel", "arbitrary"),
        name="ffn_ln",
    )(x, w_gu, w_gu, w_down, ln_g.reshape(1, d), ln_b.reshape(1, d))


def _rope_tables(seq):
    t = np.arange(seq)
    pos = np.stack([t // GRID_W, t % GRID_W], axis=1).astype(np.float32)
    lane = np.arange(HEAD_DIM)
    half = lane // (HEAD_DIM // 2)
    within = lane % (HEAD_DIM // 2)
    quarter = HEAD_DIM // 4
    first = within < quarter
    freq = within % quarter
    axis_rot = HEAD_DIM // 2
    inv = jnp.asarray(ROPE_THETA, F32) ** (
        -jnp.arange(0, axis_rot, 2, dtype=F32) / axis_rot)
    ang = jnp.asarray(pos)[:, half] * inv[freq][None, :]
    cos = jnp.cos(ang)
    sin = jnp.sin(ang)
    first = jnp.asarray(first)[None, :]
    sin_up = jnp.where(first, -sin, 0.0)
    sin_dn = jnp.where(first, 0.0, sin)
    return cos, sin_up, sin_dn


def _inproj_kernel(x_ref, w_ref, qg_ref, kg_ref, cos_ref, sup_ref, sdn_ref, o_ref):
    xb = x_ref[...].astype(BF16)
    cos = cos_ref[...]
    sup = sup_ref[...]
    sdn = sdn_ref[...]
    scale = HEAD_DIM ** -0.5
    quarter = HEAD_DIM // 4
    pair = 2 * HEAD_DIM

    def norm_rope(z, gain):
        z = z * lax.rsqrt(jnp.mean(z * z, axis=-1, keepdims=True) + RMS_EPS) * gain
        return (z * cos
                + pltpu.roll(z, HEAD_DIM - quarter, 1) * sup
                + pltpu.roll(z, quarter, 1) * sdn)

    for c in range(0, W_ATTN_COLS, pair):
        z2 = _dot(xb, w_ref[:, c:c + pair])
        for hh in range(2):
            col = c + hh * HEAD_DIM
            z = z2[:, hh * HEAD_DIM:(hh + 1) * HEAD_DIM]
            if col < COL_KA:
                z = norm_rope(z, qg_ref[...]) * scale
            elif col < COL_VA:
                z = norm_rope(z, kg_ref[...])
            elif COL_QB <= col < COL_KB or col >= COL_QM:
                z = z * scale
            o_ref[:, col:col + HEAD_DIM] = z.astype(BF16)


def _inproj(x, w_in, q_gain, k_gain, seq, *, tm=512):
    t, d = x.shape
    assert t % tm == 0 and seq % tm == 0
    cos, sup, sdn = _rope_tables(seq)
    nseq = seq // tm
    tab_spec = pl.BlockSpec((tm, HEAD_DIM), lambda i: (i % nseq, 0))
    gain_spec = pl.BlockSpec((1, HEAD_DIM), lambda i: (0, 0))
    return pl.pallas_call(
        _inproj_kernel,
        grid=(t // tm,),
        in_specs=[
            pl.BlockSpec((tm, d), lambda i: (i, 0)),
            pl.BlockSpec((d, W_ATTN_COLS), lambda i: (0, 0), pipeline_mode=pl.Buffered(1)),
            gain_spec, gain_spec, tab_spec, tab_spec, tab_spec,
        ],
        out_specs=pl.BlockSpec((tm, W_ATTN_COLS), lambda i: (i, 0)),
        out_shape=jax.ShapeDtypeStruct((t, W_ATTN_COLS), BF16),
        compiler_params=_params("parallel"),
        name="inproj",
    )(x, w_in, q_gain.reshape(1, HEAD_DIM), k_gain.reshape(1, HEAD_DIM), cos, sup, sdn)


def _memkv_kernel(m_ref, w_ref, o_ref):
    o_ref[...] = _dot(m_ref[...].astype(BF16), w_ref[...]).astype(BF16)


def _memkv(mem2d, w, *, tm=512):
    t, d = mem2d.shape
    n = w.shape[1]
    assert t % tm == 0
    return pl.pallas_call(
        _memkv_kernel,
        grid=(t // tm,),
        in_specs=[pl.BlockSpec((tm, d), lambda i: (i, 0)),
                  pl.BlockSpec((d, n), lambda i: (0, 0))],
        out_specs=pl.BlockSpec((tm, n), lambda i: (i, 0)),
        out_shape=jax.ShapeDtypeStruct((t, n), BF16),
        compiler_params=_params("parallel"),
        name="memkv",
    )(mem2d, w)


def _with_ones_column(v):
    lane = lax.broadcasted_iota(jnp.int32, v.shape, 1)
    ones_col = jnp.where(lane == 0, 1.0, 0.0).astype(v.dtype)
    return jnp.concatenate([v, ones_col], axis=1)


def _softmax_pv(s, v_ext):
    m = jnp.max(s, axis=-1, keepdims=True)
    p = jnp.exp(s - m).astype(BF16)
    ov = _dot(p, v_ext)
    return ov[:, :HEAD_DIM] / ov[:, HEAD_DIM:HEAD_DIM + 1]


def _softmax_pv_few_keys(s, v):
    m = jnp.max(s, axis=-1, keepdims=True)
    p = jnp.exp(s - m)
    l = jnp.sum(p, axis=-1, keepdims=True)
    return _dot(p.astype(BF16), v) / l


GQA_GROUP = GQA_Q_HEADS // GQA_KV_HEADS
GQA_SUB_ROWS = 256


def _gqa_kernel(q_ref, k_ref, v_ref, o_ref):
    v_ext = _with_ones_column(v_ref[...])
    for r in range(0, q_ref.shape[0], GQA_SUB_ROWS):
        rows = slice(r, r + GQA_SUB_ROWS)
        for h in range(GQA_GROUP):
            cols = slice(h * HEAD_DIM, (h + 1) * HEAD_DIM)
            s = _dot_nt(q_ref[rows, cols], k_ref[...])
            o_ref[rows, cols] = _softmax_pv(s, v_ext).astype(BF16)


def _gqa(z, batch, seq, *, tq=1024):
    t = z.shape[0]
    nq = seq // tq
    gw = GQA_GROUP * HEAD_DIM
    kcol = COL_KA // HEAD_DIM
    vcol = COL_VA // HEAD_DIM
    return pl.pallas_call(
        _gqa_kernel,
        grid=(batch, GQA_KV_HEADS, nq),
        in_specs=[
            pl.BlockSpec((tq, gw), lambda b, g, i: (b * nq + i, g)),
            pl.BlockSpec((seq, HEAD_DIM), lambda b, g, i: (b, kcol + g)),
            pl.BlockSpec((seq, HEAD_DIM), lambda b, g, i: (b, vcol + g)),
        ],
        out_specs=pl.BlockSpec((tq, gw), lambda b, g, i: (b * nq + i, g)),
        out_shape=jax.ShapeDtypeStruct((t, WA_Q), BF16),
        compiler_params=_params("parallel", "parallel", "parallel"),
        name="gqa",
    )(z, z, z)


NA_Q_ROWS = 4
NA_K_ROWS = NA_Q_ROWS + NA_ROWS_MAX
NA_TQ = NA_Q_ROWS * GRID_W
NA_TK = NA_K_ROWS * GRID_W


def _na_key_row_start(qblock, rows):
    return np.clip(qblock * NA_Q_ROWS - NA_ROWS_MAX // 2, 0, rows - NA_K_ROWS)


def _na_bias_tables(rpb, rows):
    heads, n_dr, n_dc = rpb.shape
    nblocks = rows // NA_Q_ROWS
    kr_win = min(NA_ROWS_MAX, rows)
    period = 2 * GRID_W
    assert period >= GRID_W + NA_COLS and n_dc == 2 * NA_COLS - 1
    v = rpb.astype(F32)
    vpad = jnp.concatenate(
        [v[..., NA_COLS - 1:], jnp.zeros((heads, n_dr, period - n_dc), F32), v[..., :NA_COLS - 1]],
        axis=-1)
    skew = jnp.tile(vpad, (1, 1, GRID_W))[..., :GRID_W * (period - 1)]
    toep = skew.reshape(heads, n_dr, GRID_W, period - 1)[..., :GRID_W]
    qc = np.arange(GRID_W)
    cs = np.clip(qc - NA_COLS // 2, 0, GRID_W - NA_COLS)
    ok_c = (qc[None, :] >= cs[:, None]) & (qc[None, :] < cs[:, None] + NA_COLS)
    toep = jnp.where(jnp.asarray(ok_c)[None, None], toep, MASK_VALUE)
    masked = jnp.full((heads, GRID_W, GRID_W), MASK_VALUE, F32)
    tabs = []
    for qblock in (0, 1, nblocks - 1):
        ks = _na_key_row_start(qblock, rows)
        q_rows = []
        for qi in range(NA_Q_ROWS):
            qr = qblock * NA_Q_ROWS + qi
            rs = np.clip(qr - kr_win // 2, 0, rows - kr_win)
            blocks = []
            for kj in range(NA_K_ROWS):
                kr = ks + kj
                inside = rs <= kr < rs + kr_win
                blocks.append(toep[:, kr - qr + NA_ROWS_MAX - 1] if inside else masked)
            q_rows.append(jnp.concatenate(blocks, axis=-1))
        tabs.append(jnp.concatenate(q_rows, axis=-2))
    return jnp.stack(tabs)


def _na_kernel(q_ref, k_ref, v_ref, bias_ref, o_ref):
    rows = q_ref.shape[0] // GRID_W
    n_blocks = rows // NA_Q_ROWS
    for i in range(n_blocks):
        pattern = 0 if i == 0 else (2 if i == n_blocks - 1 else 1)
        qrows = slice(i * NA_TQ, (i + 1) * NA_TQ)
        start = int(_na_key_row_start(i, rows)) * GRID_W
        krows = slice(start, start + NA_TK)
        for h in range(NA_HEADS):
            cols = slice(h * HEAD_DIM, (h + 1) * HEAD_DIM)
            s = _dot_nt(q_ref[qrows, cols], k_ref[krows, cols]) + bias_ref[pattern, h]
            v_ext = _with_ones_column(v_ref[krows, cols])
            o_ref[qrows, cols] = _softmax_pv(s, v_ext).astype(BF16)


def _na(z, rpb, batch, seq):
    t = z.shape[0]
    rows = seq // GRID_W
    assert rows % NA_Q_ROWS == 0 and rows >= NA_K_ROWS
    bias = _na_bias_tables(rpb, rows)
    return pl.pallas_call(
        _na_kernel,
        grid=(batch,),
        in_specs=[
            pl.BlockSpec((seq, WB), lambda b: (b, COL_QB // WB)),
            pl.BlockSpec((seq, WB), lambda b: (b, COL_KB // WB)),
            pl.BlockSpec((seq, WB), lambda b: (b, COL_VB // WB)),
            pl.BlockSpec(bias.shape, lambda b: (0, 0, 0, 0), pipeline_mode=pl.Buffered(1)),
        ],
        out_specs=pl.BlockSpec((seq, WB), lambda b: (b, 0)),
        out_shape=jax.ShapeDtypeStruct((t, WB), BF16),
        compiler_params=_params("parallel"),
        name="natten",
    )(z, z, z, bias)


def _mem_attn_kernel(q_ref, kv_ref, o_ref):
    for h in range(MEM_HEADS):
        cols = slice(h * HEAD_DIM, (h + 1) * HEAD_DIM)
        s = _dot_nt(q_ref[:, cols], kv_ref[:, cols])
        v = kv_ref[:, WM + h * HEAD_DIM:WM + (h + 1) * HEAD_DIM]
        o_ref[:, cols] = _softmax_pv_few_keys(s, v).astype(BF16)


def _mem_attn(z, kvm, batch, seq, mem_len, *, tq=512):
    t = z.shape[0]
    nq = seq // tq
    return pl.pallas_call(
        _mem_attn_kernel,
        grid=(batch, nq),
        in_specs=[
            pl.BlockSpec((tq, WM), lambda b, i: (b * nq + i, COL_QM // WM)),
            pl.BlockSpec((mem_len, 2 * WM), lambda b, i: (b, 0)),
        ],
        out_specs=pl.BlockSpec((tq, WM), lambda b, i: (b * nq + i, 0)),
        out_shape=jax.ShapeDtypeStruct((t, WM), BF16),
        compiler_params=_params("parallel", "parallel"),
        name="mem_attn",
    )(z, kvm)


MXU_COLS = 256


def _gated_merge_kernel(x_ref, oa_ref, ob_ref, om_ref, wg_ref, bg_ref,
                        woa_ref, wob_ref, wom_ref, y_ref):
    d = x_ref.shape[1]
    xb = x_ref[...].astype(BF16)
    branches = ((oa_ref[...], woa_ref), (ob_ref[...], wob_ref), (om_ref[...], wom_ref))
    for c in range(0, d, MXU_COLS):
        cols = slice(c, c + MXU_COLS)
        y = None
        for k, (o, wo_ref) in enumerate(branches):
            gcols = slice(k * d + c, k * d + c + MXU_COLS)
            gate = jax.nn.sigmoid(_dot(xb, wg_ref[:, gcols]) + bg_ref[:, gcols])
            term = gate * _dot(o, wo_ref[:, cols])
            y = term if y is None else y + term
        y_ref[:, cols] = y.astype(BF16)


def _gated_merge(x, oa, ob, om, w_gate, b_gate, w_oa, w_ob, w_om, *, tm=512):
    t, d = x.shape
    assert t % tm == 0 and d % MXU_COLS == 0 and w_gate.shape == (d, N_BRANCHES * d)

    def rows(width):
        return pl.BlockSpec((tm, width), lambda i: (i, 0))

    def resident(arr):
        return pl.BlockSpec(arr.shape, lambda i: (0, 0), pipeline_mode=pl.Buffered(1))

    b_gate = b_gate.reshape(1, N_BRANCHES * d)
    return pl.pallas_call(
        _gated_merge_kernel,
        grid=(t // tm,),
        in_specs=[rows(d), rows(WA_Q), rows(WB), rows(WM),
                  resident(w_gate), resident(b_gate),
                  resident(w_oa), resident(w_ob), resident(w_om)],
        out_specs=rows(d),
        out_shape=jax.ShapeDtypeStruct((t, d), BF16),
        compiler_params=_params("parallel"),
        name="gated_merge",
    )(x, oa, ob, om, w_gate, b_gate, w_oa, w_ob, w_om)


def _outproj_ln_kernel(x_ref, y_ref, w_ref, g_ref, b_ref, o_ref, *, alpha):
    tm = x_ref.shape[0]
    g = g_ref[...]
    b = b_ref[...]
    for r in range(0, tm, LN_OVERLAP_ROWS):
        rows = slice(r, r + LN_OVERLAP_ROWS)
        y = alpha * x_ref[rows, :] + _dot(y_ref[rows, :], w_ref[...])
        o_ref[rows, :] = _layer_norm_rows(y, g, b)


def _outproj_ln(x, y, w_out, ln_g, ln_b, *, alpha, tm=1024):
    t, d = x.shape
    assert t % tm == 0
    vec = pl.BlockSpec((1, d), lambda i: (0, 0))
    return pl.pallas_call(
        functools.partial(_outproj_ln_kernel, alpha=alpha),
        grid=(t // tm,),
        in_specs=[
            pl.BlockSpec((tm, d), lambda i: (i, 0)),
            pl.BlockSpec((tm, d), lambda i: (i, 0)),
            pl.BlockSpec((d, d), lambda i: (0, 0), pipeline_mode=pl.Buffered(1)),
            vec, vec,
        ],
        out_specs=pl.BlockSpec((tm, d), lambda i: (i, 0)),
        out_shape=jax.ShapeDtypeStruct((t, d), F32),
        compiler_params=_params("parallel"),
        name="outproj_ln",
    )(x, y, w_out, ln_g.reshape(1, d), ln_b.reshape(1, d))


def kernel(x, mem, ln1_g, ln1_b, ffn1_w_gu, ffn1_w_down, w_in, b_gate, q_norm_a, k_norm_a,
           na_rpb, w_mem_kv, w_oa, w_ob, w_om, w_out, ln2_g, ln2_b, ffn2_w_gu, ffn2_w_down,
           ln3_g, ln3_b):
    batch, seq, d = x.shape
    mem_len = mem.shape[1]
    depth = w_in.shape[0]
    alpha = (2 * depth) ** 0.25
    h = x.reshape(batch * seq, d)
    mem2d = mem.reshape(batch * mem_len, d)
    for l in range(depth):
        bf = lambda w: w[l].astype(BF16)
        h = _ffn_ln(h, bf(ffn1_w_gu), bf(ffn1_w_down), ln1_g[l], ln1_b[l], alpha=alpha)
        w_attn = w_in[l][:, :W_ATTN_COLS].astype(BF16)
        w_gate = w_in[l][:, W_ATTN_COLS:].astype(BF16)
        z = _inproj(h, w_attn, q_norm_a[l], k_norm_a[l], seq)
        kvm = _memkv(mem2d, bf(w_mem_kv))
        oa = _gqa(z, batch, seq)
        ob = _na(z, na_rpb[l], batch, seq)
        om = _mem_attn(z, kvm, batch, seq, mem_len)
        y = _gated_merge(h, oa, ob, om, w_gate, b_gate[l], bf(w_oa), bf(w_ob), bf(w_om))
        h = _outproj_ln(h, y, bf(w_out), ln2_g[l], ln2_b[l], alpha=alpha)
        h = _ffn_ln(h, bf(ffn2_w_gu), bf(ffn2_w_down), ln3_g[l], ln3_b[l], alpha=alpha)
    return h.reshape(batch, seq, d)
```

```python
import functools

import numpy as np
import jax
import jax.numpy as jnp
from jax import lax
from jax.experimental import pallas as pl
from jax.experimental.pallas import tpu as pltpu

HEAD_DIM = 128
GQA_Q_HEADS = 8
GQA_KV_HEADS = 2
NA_HEADS = 4
MEM_HEADS = 4
GRID_W = 64
NA_ROWS_MAX = 8
NA_COLS = 16
ROPE_THETA = 10000.0
LN_EPS = 1e-5
RMS_EPS = 1e-6
N_BRANCHES = 3
MASK_VALUE = -1e30

WA_Q = GQA_Q_HEADS * HEAD_DIM
WA_KV = GQA_KV_HEADS * HEAD_DIM
WB = NA_HEADS * HEAD_DIM
WM = MEM_HEADS * HEAD_DIM
COL_KA = WA_Q
COL_VA = COL_KA + WA_KV
COL_QB = COL_VA + WA_KV
COL_KB = COL_QB + WB
COL_VB = COL_KB + WB
COL_QM = COL_VB + WB
W_ATTN_COLS = COL_QM + WM

V7X_VMEM_BYTES = 64 * 1024 * 1024
VMEM_LIMIT_BYTES = V7X_VMEM_BYTES - 8 * 1024 * 1024

BF16 = jnp.bfloat16
F32 = jnp.float32


def _params(*semantics):
    return pltpu.CompilerParams(dimension_semantics=semantics,
                                vmem_limit_bytes=VMEM_LIMIT_BYTES)


def _dot(a, b):
    return jnp.dot(a, b, preferred_element_type=F32)


def _dot_nt(a, b):
    return lax.dot_general(a, b, (((1,), (1,)), ((), ())), preferred_element_type=F32)


def _layer_norm_rows(y, g, b):
    mu = jnp.mean(y, axis=-1, keepdims=True)
    yc = y - mu
    var = jnp.mean(yc * yc, axis=-1, keepdims=True)
    return yc * lax.rsqrt(var + LN_EPS) * g + b


LN_ROW_CHUNK = 128


LN_OVERLAP_ROWS = 256


def _for_row_chunks(n_rows, body):
    def step(r, carry):
        body(pl.ds(pl.multiple_of(r * LN_ROW_CHUNK, LN_ROW_CHUNK), LN_ROW_CHUNK))
        return carry
    lax.fori_loop(0, n_rows // LN_ROW_CHUNK, step, 0)


def _ffn_ln_kernel(x_ref, wg_ref, wu_ref, wd_ref, g_ref, b_ref, o_ref, xb_ref, *, alpha):
    j = pl.program_id(1)
    tm = x_ref.shape[0]

    @pl.when(j == 0)
    def _init():
        def body(rows):
            x = x_ref[rows, :]
            xb_ref[rows, :] = x.astype(BF16)
            o_ref[rows, :] = alpha * x
        _for_row_chunks(tm, body)

    xb = xb_ref[...]
    gate = _dot(xb, wg_ref[...])
    up = _dot(xb, wu_ref[...])
    h = (gate * jax.nn.sigmoid(gate) * (0.5 * up)).astype(BF16)
    o_ref[...] += _dot(h, wd_ref[...])

    @pl.when(j == pl.num_programs(1) - 1)
    def _finish():
        g = g_ref[...]
        b = b_ref[...]

        def body(rows):
            o_ref[rows, :] = _layer_norm_rows(o_ref[rows, :], g, b)
        _for_row_chunks(tm, body)


def _ffn_ln(x, w_gu, w_down, ln_g, ln_b, *, alpha, tm=1024, tf=512):
    t, d = x.shape
    f = w_down.shape[0]
    assert t % tm == 0 and f % tf == 0
    nf = f // tf
    return pl.pallas_call(
        functools.partial(_ffn_ln_kernel, alpha=alpha),
        grid=(t // tm, nf),
        in_specs=[
            pl.BlockSpec((tm, d), lambda i, j: (i, 0)),
            pl.BlockSpec((d, tf), lambda i, j: (0, j)),
            pl.BlockSpec((d, tf), lambda i, j: (0, j + nf)),
            pl.BlockSpec((tf, d), lambda i, j: (j, 0)),
            pl.BlockSpec((1, d), lambda i, j: (0, 0)),
            pl.BlockSpec((1, d), lambda i, j: (0, 0)),
        ],
        out_specs=pl.BlockSpec((tm, d), lambda i, j: (i, 0)),
        out_shape=jax.ShapeDtypeStruct((t, d), F32),
        scratch_shapes=[pltpu.VMEM((tm, d), BF16)],
        compiler_params=_params("parallel", "arbitrary"),
        name="ffn_ln",
    )(x, w_gu, w_gu, w_down, ln_g.reshape(1, d), ln_b.reshape(1, d))


def _rope_tables(seq):
    t = np.arange(seq)
    pos = np.stack([t // GRID_W, t % GRID_W], axis=1).astype(np.float32)
    lane = np.arange(HEAD_DIM)
    half = lane // (HEAD_DIM // 2)
    within = lane % (HEAD_DIM // 2)
    quarter = HEAD_DIM // 4
    first = within < quarter
    freq = within % quarter
    axis_rot = HEAD_DIM // 2
    inv = jnp.asarray(ROPE_THETA, F32) ** (
        -jnp.arange(0, axis_rot, 2, dtype=F32) / axis_rot)
    ang = jnp.asarray(pos)[:, half] * inv[freq][None, :]
    cos = jnp.cos(ang)
    sin = jnp.sin(ang)
    first = jnp.asarray(first)[None, :]
    sin_up = jnp.where(first, -sin, 0.0)
    sin_dn = jnp.where(first, 0.0, sin)
    return cos, sin_up, sin_dn


def _inproj_kernel(x_ref, w_ref, qg_ref, kg_ref, cos_ref, sup_ref, sdn_ref, wfull_ref,
                   o_ref, wgate_ref):
    wgate_ref[...] = wfull_ref[:, W_ATTN_COLS:].astype(BF16)
    xb = x_ref[...].astype(BF16)
    cos = cos_ref[...]
    sup = sup_ref[...]
    sdn = sdn_ref[...]
    scale = HEAD_DIM ** -0.5
    quarter = HEAD_DIM // 4
    pair = 2 * HEAD_DIM

    def norm_rope(z, gain):
        z = z * lax.rsqrt(jnp.mean(z * z, axis=-1, keepdims=True) + RMS_EPS) * gain
        return (z * cos
                + pltpu.roll(z, HEAD_DIM - quarter, 1) * sup
                + pltpu.roll(z, quarter, 1) * sdn)

    for c in range(0, W_ATTN_COLS, pair):
        z2 = _dot(xb, w_ref[:, c:c + pair])
        for hh in range(2):
            col = c + hh * HEAD_DIM
            z = z2[:, hh * HEAD_DIM:(hh + 1) * HEAD_DIM]
            if col < COL_KA:
                z = norm_rope(z, qg_ref[...]) * scale
            elif col < COL_VA:
                z = norm_rope(z, kg_ref[...])
            elif COL_QB <= col < COL_KB or col >= COL_QM:
                z = z * scale
            o_ref[:, col:col + HEAD_DIM] = z.astype(BF16)


def _inproj(x, w_attn, w_full, q_gain, k_gain, seq, *, tm=512):
    t, d = x.shape
    assert t % tm == 0 and seq % tm == 0
    cos, sup, sdn = _rope_tables(seq)
    nseq = seq // tm
    n_steps = t // tm
    slab = d // n_steps
    assert slab * n_steps == d and slab % 16 == 0
    n_gate = w_full.shape[1] - W_ATTN_COLS
    tab_spec = pl.BlockSpec((tm, HEAD_DIM), lambda i: (i % nseq, 0))
    gain_spec = pl.BlockSpec((1, HEAD_DIM), lambda i: (0, 0))
    return pl.pallas_call(
        _inproj_kernel,
        grid=(n_steps,),
        in_specs=[
            pl.BlockSpec((tm, d), lambda i: (i, 0)),
            pl.BlockSpec((d, W_ATTN_COLS), lambda i: (0, 0), pipeline_mode=pl.Buffered(1)),
            gain_spec, gain_spec, tab_spec, tab_spec, tab_spec,
            pl.BlockSpec((slab, w_full.shape[1]), lambda i: (i, 0)),
        ],
        out_specs=[pl.BlockSpec((tm, W_ATTN_COLS), lambda i: (i, 0)),
                   pl.BlockSpec((slab, n_gate), lambda i: (i, 0))],
        out_shape=[jax.ShapeDtypeStruct((t, W_ATTN_COLS), BF16),
                   jax.ShapeDtypeStruct((d, n_gate), BF16)],
        compiler_params=_params("arbitrary"),
        name="inproj",
    )(x, w_attn, q_gain.reshape(1, HEAD_DIM), k_gain.reshape(1, HEAD_DIM), cos, sup, sdn,
      w_full)


def _memkv_kernel(m_ref, w_ref, o_ref):
    o_ref[...] = _dot(m_ref[...].astype(BF16), w_ref[...].astype(BF16)).astype(BF16)


def _memkv(mem2d, w, *, tm=512):
    t, d = mem2d.shape
    n = w.shape[1]
    assert t % tm == 0
    return pl.pallas_call(
        _memkv_kernel,
        grid=(t // tm,),
        in_specs=[pl.BlockSpec((tm, d), lambda i: (i, 0)),
                  pl.BlockSpec((d, n), lambda i: (0, 0))],
        out_specs=pl.BlockSpec((tm, n), lambda i: (i, 0)),
        out_shape=jax.ShapeDtypeStruct((t, n), BF16),
        compiler_params=_params("parallel"),
        name="memkv",
    )(mem2d, w)


def _with_ones_column(v):
    lane = lax.broadcasted_iota(jnp.int32, v.shape, 1)
    ones_col = jnp.where(lane == 0, 1.0, 0.0).astype(v.dtype)
    return jnp.concatenate([v, ones_col], axis=1)


def _softmax_pv(s, v_ext):
    m = jnp.max(s, axis=-1, keepdims=True)
    p = jnp.exp(s - m).astype(BF16)
    ov = _dot(p, v_ext)
    return ov[:, :HEAD_DIM] / ov[:, HEAD_DIM:HEAD_DIM + 1]


def _softmax_pv_few_keys(s, v):
    m = jnp.max(s, axis=-1, keepdims=True)
    p = jnp.exp(s - m)
    l = jnp.sum(p, axis=-1, keepdims=True)
    return _dot(p.astype(BF16), v) / l


GQA_GROUP = GQA_Q_HEADS // GQA_KV_HEADS
GQA_SUB_ROWS = 256


def _gqa_kernel(q_ref, k_ref, v_ref, *refs):
    n_cast = (len(refs) - 1) // 2
    o_ref = refs[n_cast]
    v_ext = _with_ones_column(v_ref[...])
    for r in range(0, q_ref.shape[0], GQA_SUB_ROWS):
        rows = slice(r, r + GQA_SUB_ROWS)
        for h in range(GQA_GROUP):
            cols = slice(h * HEAD_DIM, (h + 1) * HEAD_DIM)
            s = _dot_nt(q_ref[rows, cols], k_ref[...])
            o_ref[rows, cols] = _softmax_pv(s, v_ext).astype(BF16)
    for src_ref, dst_ref in zip(refs[:n_cast], refs[n_cast + 1:]):
        dst_ref[...] = src_ref[...].astype(BF16)


def _gqa(z, batch, seq, cast_along=(), *, tq=1024):
    t = z.shape[0]
    nq = seq // tq
    gw = GQA_GROUP * HEAD_DIM
    kcol = COL_KA // HEAD_DIM
    vcol = COL_VA // HEAD_DIM
    n_steps = batch * GQA_KV_HEADS * nq

    def slab(arr):
        rows = arr.shape[0] // n_steps
        assert rows * n_steps == arr.shape[0] and rows % 16 == 0
        return pl.BlockSpec((rows, arr.shape[1]),
                            lambda b, g, i: ((b * GQA_KV_HEADS + g) * nq + i, 0))

    slabs = [slab(a) for a in cast_along]
    return pl.pallas_call(
        _gqa_kernel,
        grid=(batch, GQA_KV_HEADS, nq),
        in_specs=[
            pl.BlockSpec((tq, gw), lambda b, g, i: (b * nq + i, g)),
            pl.BlockSpec((seq, HEAD_DIM), lambda b, g, i: (b, kcol + g)),
            pl.BlockSpec((seq, HEAD_DIM), lambda b, g, i: (b, vcol + g)),
        ] + slabs,
        out_specs=[pl.BlockSpec((tq, gw), lambda b, g, i: (b * nq + i, g))] + slabs,
        out_shape=[jax.ShapeDtypeStruct((t, WA_Q), BF16)]
        + [jax.ShapeDtypeStruct(a.shape, BF16) for a in cast_along],
        compiler_params=_params("arbitrary", "arbitrary", "arbitrary"),
        name="gqa",
    )(z, z, z, *cast_along)


NA_Q_ROWS = 4
NA_K_ROWS = NA_Q_ROWS + NA_ROWS_MAX
NA_TQ = NA_Q_ROWS * GRID_W
NA_TK = NA_K_ROWS * GRID_W


def _na_key_row_start(qblock, rows):
    return np.clip(qblock * NA_Q_ROWS - NA_ROWS_MAX // 2, 0, rows - NA_K_ROWS)


def _na_bias_tables(rpb, rows):
    heads, n_dr, n_dc = rpb.shape
    nblocks = rows // NA_Q_ROWS
    kr_win = min(NA_ROWS_MAX, rows)
    period = 2 * GRID_W
    assert period >= GRID_W + NA_COLS and n_dc == 2 * NA_COLS - 1
    v = rpb.astype(F32)
    vpad = jnp.concatenate(
        [v[..., NA_COLS - 1:], jnp.zeros((heads, n_dr, period - n_dc), F32), v[..., :NA_COLS - 1]],
        axis=-1)
    skew = jnp.tile(vpad, (1, 1, GRID_W))[..., :GRID_W * (period - 1)]
    toep = skew.reshape(heads, n_dr, GRID_W, period - 1)[..., :GRID_W]
    qc = np.arange(GRID_W)
    cs = np.clip(qc - NA_COLS // 2, 0, GRID_W - NA_COLS)
    ok_c = (qc[None, :] >= cs[:, None]) & (qc[None, :] < cs[:, None] + NA_COLS)
    toep = jnp.where(jnp.asarray(ok_c)[None, None], toep, MASK_VALUE)
    masked = jnp.full((heads, GRID_W, GRID_W), MASK_VALUE, F32)
    tabs = []
    for qblock in (0, 1, nblocks - 1):
        ks = _na_key_row_start(qblock, rows)
        q_rows = []
        for qi in range(NA_Q_ROWS):
            qr = qblock * NA_Q_ROWS + qi
            rs = np.clip(qr - kr_win // 2, 0, rows - kr_win)
            blocks = []
            for kj in range(NA_K_ROWS):
                kr = ks + kj
                inside = rs <= kr < rs + kr_win
                blocks.append(toep[:, kr - qr + NA_ROWS_MAX - 1] if inside else masked)
            q_rows.append(jnp.concatenate(blocks, axis=-1))
        tabs.append(jnp.concatenate(q_rows, axis=-2))
    return jnp.stack(tabs)


def _na_kernel(q_ref, k_ref, v_ref, bias_ref, o_ref):
    rows = q_ref.shape[0] // GRID_W
    n_blocks = rows // NA_Q_ROWS
    for i in range(n_blocks):
        pattern = 0 if i == 0 else (2 if i == n_blocks - 1 else 1)
        qrows = slice(i * NA_TQ, (i + 1) * NA_TQ)
        start = int(_na_key_row_start(i, rows)) * GRID_W
        krows = slice(start, start + NA_TK)
        for h in range(NA_HEADS):
            cols = slice(h * HEAD_DIM, (h + 1) * HEAD_DIM)
            s = _dot_nt(q_ref[qrows, cols], k_ref[krows, cols]) + bias_ref[pattern, h]
            v_ext = _with_ones_column(v_ref[krows, cols])
            o_ref[qrows, cols] = _softmax_pv(s, v_ext).astype(BF16)


def _na(z, rpb, batch, seq):
    t = z.shape[0]
    rows = seq // GRID_W
    assert rows % NA_Q_ROWS == 0 and rows >= NA_K_ROWS
    bias = _na_bias_tables(rpb, rows)
    return pl.pallas_call(
        _na_kernel,
        grid=(batch,),
        in_specs=[
            pl.BlockSpec((seq, WB), lambda b: (b, COL_QB // WB)),
            pl.BlockSpec((seq, WB), lambda b: (b, COL_KB // WB)),
            pl.BlockSpec((seq, WB), lambda b: (b, COL_VB // WB)),
            pl.BlockSpec(bias.shape, lambda b: (0, 0, 0, 0), pipeline_mode=pl.Buffered(1)),
        ],
        out_specs=pl.BlockSpec((seq, WB), lambda b: (b, 0)),
        out_shape=jax.ShapeDtypeStruct((t, WB), BF16),
        compiler_params=_params("parallel"),
        name="natten",
    )(z, z, z, bias)


def _mem_attn_kernel(q_ref, kv_ref, o_ref):
    for h in range(MEM_HEADS):
        cols = slice(h * HEAD_DIM, (h + 1) * HEAD_DIM)
        s = _dot_nt(q_ref[:, cols], kv_ref[:, cols])
        v = kv_ref[:, WM + h * HEAD_DIM:WM + (h + 1) * HEAD_DIM]
        o_ref[:, cols] = _softmax_pv_few_keys(s, v).astype(BF16)


def _mem_attn(z, kvm, batch, seq, mem_len, *, tq=512):
    t = z.shape[0]
    nq = seq // tq
    return pl.pallas_call(
        _mem_attn_kernel,
        grid=(batch, nq),
        in_specs=[
            pl.BlockSpec((tq, WM), lambda b, i: (b * nq + i, COL_QM // WM)),
            pl.BlockSpec((mem_len, 2 * WM), lambda b, i: (b, 0)),
        ],
        out_specs=pl.BlockSpec((tq, WM), lambda b, i: (b * nq + i, 0)),
        out_shape=jax.ShapeDtypeStruct((t, WM), BF16),
        compiler_params=_params("parallel", "parallel"),
        name="mem_attn",
    )(z, kvm)


MXU_COLS = 256


def _gated_merge_kernel(x_ref, oa_ref, ob_ref, om_ref, wg_ref, bg_ref,
                        woa_ref, wob_ref, wom_ref, y_ref):
    d = x_ref.shape[1]
    xb = x_ref[...].astype(BF16)
    branches = ((oa_ref[...], woa_ref), (ob_ref[...], wob_ref), (om_ref[...], wom_ref))
    for c in range(0, d, MXU_COLS):
        cols = slice(c, c + MXU_COLS)
        y = None
        for k, (o, wo_ref) in enumerate(branches):
            gcols = slice(k * d + c, k * d + c + MXU_COLS)
            gate = jax.nn.sigmoid(_dot(xb, wg_ref[:, gcols]) + bg_ref[:, gcols])
            term = gate * _dot(o, wo_ref[:, cols])
            y = term if y is None else y + term
        y_ref[:, cols] = y.astype(BF16)


def _gated_merge(x, oa, ob, om, w_gate, b_gate, w_oa, w_ob, w_om, *, tm=512):
    t, d = x.shape
    assert t % tm == 0 and d % MXU_COLS == 0 and w_gate.shape == (d, N_BRANCHES * d)

    def rows(width):
        return pl.BlockSpec((tm, width), lambda i: (i, 0))

    def resident(arr):
        return pl.BlockSpec(arr.shape, lambda i: (0, 0), pipeline_mode=pl.Buffered(1))

    b_gate = b_gate.reshape(1, N_BRANCHES * d)
    return pl.pallas_call(
        _gated_merge_kernel,
        grid=(t // tm,),
        in_specs=[rows(d), rows(WA_Q), rows(WB), rows(WM),
                  resident(w_gate), resident(b_gate),
                  resident(w_oa), resident(w_ob), resident(w_om)],
        out_specs=rows(d),
        out_shape=jax.ShapeDtypeStruct((t, d), BF16),
        compiler_params=_params("parallel"),
        name="gated_merge",
    )(x, oa, ob, om, w_gate, b_gate, w_oa, w_ob, w_om)


def _outproj_ln_kernel(x_ref, y_ref, w_ref, g_ref, b_ref, o_ref, *, alpha):
    tm = x_ref.shape[0]
    g = g_ref[...]
    b = b_ref[...]
    for r in range(0, tm, LN_OVERLAP_ROWS):
        rows = slice(r, r + LN_OVERLAP_ROWS)
        y = alpha * x_ref[rows, :] + _dot(y_ref[rows, :], w_ref[...])
        o_ref[rows, :] = _layer_norm_rows(y, g, b)


def _outproj_ln(x, y, w_out, ln_g, ln_b, *, alpha, tm=1024):
    t, d = x.shape
    assert t % tm == 0
    vec = pl.BlockSpec((1, d), lambda i: (0, 0))
    return pl.pallas_call(
        functools.partial(_outproj_ln_kernel, alpha=alpha),
        grid=(t // tm,),
        in_specs=[
            pl.BlockSpec((tm, d), lambda i: (i, 0)),
            pl.BlockSpec((tm, d), lambda i: (i, 0)),
            pl.BlockSpec((d, d), lambda i: (0, 0), pipeline_mode=pl.Buffered(1)),
            vec, vec,
        ],
        out_specs=pl.BlockSpec((tm, d), lambda i: (i, 0)),
        out_shape=jax.ShapeDtypeStruct((t, d), F32),
        compiler_params=_params("parallel"),
        name="outproj_ln",
    )(x, y, w_out, ln_g.reshape(1, d), ln_b.reshape(1, d))


def kernel(x, mem, ln1_g, ln1_b, ffn1_w_gu, ffn1_w_down, w_in, b_gate, q_norm_a, k_norm_a,
           na_rpb, w_mem_kv, w_oa, w_ob, w_om, w_out, ln2_g, ln2_b, ffn2_w_gu, ffn2_w_down,
           ln3_g, ln3_b):
    batch, seq, d = x.shape
    mem_len = mem.shape[1]
    depth = w_in.shape[0]
    alpha = (2 * depth) ** 0.25
    h = x.reshape(batch * seq, d)
    mem2d = mem.reshape(batch * mem_len, d)
    for l in range(depth):
        bf = lambda w: w[l].astype(BF16)
        h = _ffn_ln(h, bf(ffn1_w_gu), bf(ffn1_w_down), ln1_g[l], ln1_b[l], alpha=alpha)
        w_attn = w_in[l][:, :W_ATTN_COLS].astype(BF16)
        z, w_gate = _inproj(h, w_attn, w_in[l], q_norm_a[l], k_norm_a[l], seq)
        kvm = _memkv(mem2d, w_mem_kv[l])
        oa, w_gu2, w_down2, w_oa_b, w_ob_b, w_om_b, w_out_b = _gqa(
            z, batch, seq,
            cast_along=(ffn2_w_gu[l], ffn2_w_down[l], w_oa[l], w_ob[l], w_om[l], w_out[l]))
        ob = _na(z, na_rpb[l], batch, seq)
        om = _mem_attn(z, kvm, batch, seq, mem_len)
        y = _gated_merge(h, oa, ob, om, w_gate, b_gate[l], w_oa_b, w_ob_b, w_om_b)
        h = _outproj_ln(h, y, w_out_b, ln2_g[l], ln2_b[l], alpha=alpha)
        h = _ffn_ln(h, w_gu2, w_down2, ln3_g[l], ln3_b[l], alpha=alpha)
    return h.reshape(batch, seq, d)
```

```python
import functools

import numpy as np
import jax
import jax.numpy as jnp
from jax import lax
from jax.experimental import pallas as pl
from jax.experimental.pallas import tpu as pltpu

HEAD_DIM = 128
GQA_Q_HEADS = 8
GQA_KV_HEADS = 2
NA_HEADS = 4
MEM_HEADS = 4
GRID_W = 64
NA_ROWS_MAX = 8
NA_COLS = 16
ROPE_THETA = 10000.0
LN_EPS = 1e-5
RMS_EPS = 1e-6
N_BRANCHES = 3
MASK_VALUE = -1e30

WA_Q = GQA_Q_HEADS * HEAD_DIM
WA_KV = GQA_KV_HEADS * HEAD_DIM
WB = NA_HEADS * HEAD_DIM
WM = MEM_HEADS * HEAD_DIM
COL_KA = WA_Q
COL_VA = COL_KA + WA_KV
COL_QB = COL_VA + WA_KV
COL_KB = COL_QB + WB
COL_VB = COL_KB + WB
COL_QM = COL_VB + WB
W_ATTN_COLS = COL_QM + WM

V7X_VMEM_BYTES = 64 * 1024 * 1024
VMEM_LIMIT_BYTES = V7X_VMEM_BYTES - 8 * 1024 * 1024

BF16 = jnp.bfloat16
F32 = jnp.float32


def _params(*semantics):
    return pltpu.CompilerParams(dimension_semantics=semantics,
                                vmem_limit_bytes=VMEM_LIMIT_BYTES)


def _dot(a, b):
    return jnp.dot(a, b, preferred_element_type=F32)


def _dot_nt(a, b):
    return lax.dot_general(a, b, (((1,), (1,)), ((), ())), preferred_element_type=F32)


def _layer_norm_rows(y, g, b):
    mu = jnp.mean(y, axis=-1, keepdims=True)
    yc = y - mu
    var = jnp.mean(yc * yc, axis=-1, keepdims=True)
    return yc * lax.rsqrt(var + LN_EPS) * g + b


LN_ROW_CHUNK = 128


LN_OVERLAP_ROWS = 256


def _for_row_chunks(n_rows, body):
    def step(r, carry):
        body(pl.ds(pl.multiple_of(r * LN_ROW_CHUNK, LN_ROW_CHUNK), LN_ROW_CHUNK))
        return carry
    lax.fori_loop(0, n_rows // LN_ROW_CHUNK, step, 0)


def _ffn_ln_kernel(x_ref, wg_ref, wu_ref, wd_ref, g_ref, b_ref, o_ref, xb_ref, *, alpha):
    j = pl.program_id(1)
    tm = x_ref.shape[0]

    @pl.when(j == 0)
    def _init():
        def body(rows):
            x = x_ref[rows, :]
            xb_ref[rows, :] = x.astype(BF16)
            o_ref[rows, :] = alpha * x
        _for_row_chunks(tm, body)

    xb = xb_ref[...]
    gate = _dot(xb, wg_ref[...])
    up = _dot(xb, wu_ref[...])
    h = (gate * jax.nn.sigmoid(gate) * (0.5 * up)).astype(BF16)
    o_ref[...] += _dot(h, wd_ref[...])

    @pl.when(j == pl.num_programs(1) - 1)
    def _finish():
        g = g_ref[...]
        b = b_ref[...]

        def body(rows):
            o_ref[rows, :] = _layer_norm_rows(o_ref[rows, :], g, b)
        _for_row_chunks(tm, body)


def _ffn_ln(x, w_gu, w_down, ln_g, ln_b, *, alpha, tm=1024, tf=512):
    t, d = x.shape
    f = w_down.shape[0]
    assert t % tm == 0 and f % tf == 0
    nf = f // tf
    return pl.pallas_call(
        functools.partial(_ffn_ln_kernel, alpha=alpha),
        grid=(t // tm, nf),
        in_specs=[
            pl.BlockSpec((tm, d), lambda i, j: (i, 0)),
            pl.BlockSpec((d, tf), lambda i, j: (0, j)),
            pl.BlockSpec((d, tf), lambda i, j: (0, j + nf)),
            pl.BlockSpec((tf, d), lambda i, j: (j, 0)),
            pl.BlockSpec((1, d), lambda i, j: (0, 0)),
            pl.BlockSpec((1, d), lambda i, j: (0, 0)),
        ],
        out_specs=pl.BlockSpec((tm, d), lambda i, j: (i, 0)),
        out_shape=jax.ShapeDtypeStruct((t, d), F32),
        scratch_shapes=[pltpu.VMEM((tm, d), BF16)],
        compiler_params=_params("parallel", "arbitrary"),
        name="ffn_ln",
    )(x, w_gu, w_gu, w_down, ln_g.reshape(1, d), ln_b.reshape(1, d))


def _rope_tables(seq):
    t = np.arange(seq)
    pos = np.stack([t // GRID_W, t % GRID_W], axis=1).astype(np.float32)
    lane = np.arange(HEAD_DIM)
    half = lane // (HEAD_DIM // 2)
    within = lane % (HEAD_DIM // 2)
    quarter = HEAD_DIM // 4
    first = within < quarter
    freq = within % quarter
    axis_rot = HEAD_DIM // 2
    inv = jnp.asarray(ROPE_THETA, F32) ** (
        -jnp.arange(0, axis_rot, 2, dtype=F32) / axis_rot)
    ang = jnp.asarray(pos)[:, half] * inv[freq][None, :]
    cos = jnp.cos(ang)
    sin = jnp.sin(ang)
    first = jnp.asarray(first)[None, :]
    sin_up = jnp.where(first, -sin, 0.0)
    sin_dn = jnp.where(first, 0.0, sin)
    return cos, sin_up, sin_dn


def _inproj_kernel(x_ref, w_ref, qg_ref, kg_ref, cos_ref, sup_ref, sdn_ref, wfull_ref,
                   o_ref, wgate_ref):
    wgate_ref[...] = wfull_ref[:, W_ATTN_COLS:].astype(BF16)
    xb = x_ref[...].astype(BF16)
    cos = cos_ref[...]
    sup = sup_ref[...]
    sdn = sdn_ref[...]
    scale = HEAD_DIM ** -0.5
    quarter = HEAD_DIM // 4
    pair = 2 * HEAD_DIM

    def norm_rope(z, gain):
        z = z * lax.rsqrt(jnp.mean(z * z, axis=-1, keepdims=True) + RMS_EPS) * gain
        return (z * cos
                + pltpu.roll(z, HEAD_DIM - quarter, 1) * sup
                + pltpu.roll(z, quarter, 1) * sdn)

    for c in range(0, W_ATTN_COLS, pair):
        z2 = _dot(xb, w_ref[:, c:c + pair])
        for hh in range(2):
            col = c + hh * HEAD_DIM
            z = z2[:, hh * HEAD_DIM:(hh + 1) * HEAD_DIM]
            if col < COL_KA:
                z = norm_rope(z, qg_ref[...]) * scale
            elif col < COL_VA:
                z = norm_rope(z, kg_ref[...])
            elif COL_QB <= col < COL_KB or col >= COL_QM:
                z = z * scale
            o_ref[:, col:col + HEAD_DIM] = z.astype(BF16)


def _inproj(x, w_attn, w_full, q_gain, k_gain, seq, *, tm=512):
    t, d = x.shape
    assert t % tm == 0 and seq % tm == 0
    cos, sup, sdn = _rope_tables(seq)
    nseq = seq // tm
    n_steps = t // tm
    slab = d // n_steps
    assert slab * n_steps == d and slab % 16 == 0
    n_gate = w_full.shape[1] - W_ATTN_COLS
    tab_spec = pl.BlockSpec((tm, HEAD_DIM), lambda i: (i % nseq, 0))
    gain_spec = pl.BlockSpec((1, HEAD_DIM), lambda i: (0, 0))
    return pl.pallas_call(
        _inproj_kernel,
        grid=(n_steps,),
        in_specs=[
            pl.BlockSpec((tm, d), lambda i: (i, 0)),
            pl.BlockSpec((d, W_ATTN_COLS), lambda i: (0, 0), pipeline_mode=pl.Buffered(1)),
            gain_spec, gain_spec, tab_spec, tab_spec, tab_spec,
            pl.BlockSpec((slab, w_full.shape[1]), lambda i: (i, 0)),
        ],
        out_specs=[pl.BlockSpec((tm, W_ATTN_COLS), lambda i: (i, 0)),
                   pl.BlockSpec((slab, n_gate), lambda i: (i, 0))],
        out_shape=[jax.ShapeDtypeStruct((t, W_ATTN_COLS), BF16),
                   jax.ShapeDtypeStruct((d, n_gate), BF16)],
        compiler_params=_params("arbitrary"),
        name="inproj",
    )(x, w_attn, q_gain.reshape(1, HEAD_DIM), k_gain.reshape(1, HEAD_DIM), cos, sup, sdn,
      w_full)


def _with_ones_column(v):
    lane = lax.broadcasted_iota(jnp.int32, v.shape, 1)
    ones_col = jnp.where(lane == 0, 1.0, 0.0).astype(v.dtype)
    return jnp.concatenate([v, ones_col], axis=1)


def _softmax_pv(s, v_ext):
    m = jnp.max(s, axis=-1, keepdims=True)
    p = jnp.exp(s - m).astype(BF16)
    ov = _dot(p, v_ext)
    return ov[:, :HEAD_DIM] / ov[:, HEAD_DIM:HEAD_DIM + 1]


def _softmax_pv_few_keys(s, v):
    m = jnp.max(s, axis=-1, keepdims=True)
    p = jnp.exp(s - m)
    l = jnp.sum(p, axis=-1, keepdims=True)
    return _dot(p.astype(BF16), v) / l


GQA_GROUP = GQA_Q_HEADS // GQA_KV_HEADS
GQA_SUB_ROWS = 256


def _gqa_kernel(q_ref, k_ref, v_ref, *refs):
    n_cast = (len(refs) - 1) // 2
    o_ref = refs[n_cast]
    v_ext = _with_ones_column(v_ref[...])
    for r in range(0, q_ref.shape[0], GQA_SUB_ROWS):
        rows = slice(r, r + GQA_SUB_ROWS)
        for h in range(GQA_GROUP):
            cols = slice(h * HEAD_DIM, (h + 1) * HEAD_DIM)
            s = _dot_nt(q_ref[rows, cols], k_ref[...])
            o_ref[rows, cols] = _softmax_pv(s, v_ext).astype(BF16)
    for src_ref, dst_ref in zip(refs[:n_cast], refs[n_cast + 1:]):
        dst_ref[...] = src_ref[...].astype(BF16)


def _gqa(z, batch, seq, cast_along=(), *, tq=1024):
    t = z.shape[0]
    nq = seq // tq
    gw = GQA_GROUP * HEAD_DIM
    kcol = COL_KA // HEAD_DIM
    vcol = COL_VA // HEAD_DIM
    n_steps = batch * GQA_KV_HEADS * nq

    def slab(arr):
        rows = arr.shape[0] // n_steps
        assert rows * n_steps == arr.shape[0] and rows % 16 == 0
        return pl.BlockSpec((rows, arr.shape[1]),
                            lambda b, g, i: ((b * GQA_KV_HEADS + g) * nq + i, 0))

    slabs = [slab(a) for a in cast_along]
    return pl.pallas_call(
        _gqa_kernel,
        grid=(batch, GQA_KV_HEADS, nq),
        in_specs=[
            pl.BlockSpec((tq, gw), lambda b, g, i: (b * nq + i, g)),
            pl.BlockSpec((seq, HEAD_DIM), lambda b, g, i: (b, kcol + g)),
            pl.BlockSpec((seq, HEAD_DIM), lambda b, g, i: (b, vcol + g)),
        ] + slabs,
        out_specs=[pl.BlockSpec((tq, gw), lambda b, g, i: (b * nq + i, g))] + slabs,
        out_shape=[jax.ShapeDtypeStruct((t, WA_Q), BF16)]
        + [jax.ShapeDtypeStruct(a.shape, BF16) for a in cast_along],
        compiler_params=_params("arbitrary", "arbitrary", "arbitrary"),
        name="gqa",
    )(z, z, z, *cast_along)


NA_Q_ROWS = 4
NA_K_ROWS = NA_Q_ROWS + NA_ROWS_MAX
NA_TQ = NA_Q_ROWS * GRID_W
NA_TK = NA_K_ROWS * GRID_W


def _na_key_row_start(qblock, rows):
    return np.clip(qblock * NA_Q_ROWS - NA_ROWS_MAX // 2, 0, rows - NA_K_ROWS)


def _na_bias_tables(rpb, rows):
    heads, n_dr, n_dc = rpb.shape
    nblocks = rows // NA_Q_ROWS
    kr_win = min(NA_ROWS_MAX, rows)
    period = 2 * GRID_W
    assert period >= GRID_W + NA_COLS and n_dc == 2 * NA_COLS - 1
    v = rpb.astype(F32)
    vpad = jnp.concatenate(
        [v[..., NA_COLS - 1:], jnp.zeros((heads, n_dr, period - n_dc), F32), v[..., :NA_COLS - 1]],
        axis=-1)
    skew = jnp.tile(vpad, (1, 1, GRID_W))[..., :GRID_W * (period - 1)]
    toep = skew.reshape(heads, n_dr, GRID_W, period - 1)[..., :GRID_W]
    qc = np.arange(GRID_W)
    cs = np.clip(qc - NA_COLS // 2, 0, GRID_W - NA_COLS)
    ok_c = (qc[None, :] >= cs[:, None]) & (qc[None, :] < cs[:, None] + NA_COLS)
    toep = jnp.where(jnp.asarray(ok_c)[None, None], toep, MASK_VALUE)
    masked = jnp.full((heads, GRID_W, GRID_W), MASK_VALUE, F32)
    tabs = []
    for qblock in (0, 1, nblocks - 1):
        ks = _na_key_row_start(qblock, rows)
        q_rows = []
        for qi in range(NA_Q_ROWS):
            qr = qblock * NA_Q_ROWS + qi
            rs = np.clip(qr - kr_win // 2, 0, rows - kr_win)
            blocks = []
            for kj in range(NA_K_ROWS):
                kr = ks + kj
                inside = rs <= kr < rs + kr_win
                blocks.append(toep[:, kr - qr + NA_ROWS_MAX - 1] if inside else masked)
            q_rows.append(jnp.concatenate(blocks, axis=-1))
        tabs.append(jnp.concatenate(q_rows, axis=-2))
    return jnp.stack(tabs)


MEM_Q_ROWS = 512


def _local_attn_kernel(qb_ref, kb_ref, vb_ref, bias_ref, qm_ref, mem_ref, wkv_ref,
                       ob_ref, om_ref):
    rows = qb_ref.shape[0] // GRID_W
    n_blocks = rows // NA_Q_ROWS
    for i in range(n_blocks):
        pattern = 0 if i == 0 else (2 if i == n_blocks - 1 else 1)
        qrows = slice(i * NA_TQ, (i + 1) * NA_TQ)
        start = int(_na_key_row_start(i, rows)) * GRID_W
        krows = slice(start, start + NA_TK)
        for h in range(NA_HEADS):
            cols = slice(h * HEAD_DIM, (h + 1) * HEAD_DIM)
            s = _dot_nt(qb_ref[qrows, cols], kb_ref[krows, cols]) + bias_ref[pattern, h]
            v_ext = _with_ones_column(vb_ref[krows, cols])
            ob_ref[qrows, cols] = _softmax_pv(s, v_ext).astype(BF16)
    kvm = _dot(mem_ref[...].astype(BF16), wkv_ref[...].astype(BF16)).astype(BF16)
    for r in range(0, qm_ref.shape[0], MEM_Q_ROWS):
        qrows = slice(r, r + MEM_Q_ROWS)
        for h in range(MEM_HEADS):
            cols = slice(h * HEAD_DIM, (h + 1) * HEAD_DIM)
            s = _dot_nt(qm_ref[qrows, cols], kvm[:, cols])
            v = kvm[:, WM + h * HEAD_DIM:WM + (h + 1) * HEAD_DIM]
            om_ref[qrows, cols] = _softmax_pv_few_keys(s, v).astype(BF16)


def _local_attn(z, rpb, mem2d, w_mem_kv, batch, seq):
    t = z.shape[0]
    mem_len = mem2d.shape[0] // batch
    rows = seq // GRID_W
    assert rows % NA_Q_ROWS == 0 and rows >= NA_K_ROWS and seq % MEM_Q_ROWS == 0
    bias = _na_bias_tables(rpb, rows)

    def per_batch(col_block):
        return pl.BlockSpec((seq, WB), lambda b: (b, col_block))

    def resident(arr):
        return pl.BlockSpec(arr.shape, lambda b: (0,) * arr.ndim, pipeline_mode=pl.Buffered(1))

    return pl.pallas_call(
        _local_attn_kernel,
        grid=(batch,),
        in_specs=[
            per_batch(COL_QB // WB), per_batch(COL_KB // WB), per_batch(COL_VB // WB),
            resident(bias),
            pl.BlockSpec((seq, WM), lambda b: (b, COL_QM // WM)),
            pl.BlockSpec((mem_len, mem2d.shape[1]), lambda b: (b, 0)),
            resident(w_mem_kv),
        ],
        out_specs=[pl.BlockSpec((seq, WB), lambda b: (b, 0)),
                   pl.BlockSpec((seq, WM), lambda b: (b, 0))],
        out_shape=[jax.ShapeDtypeStruct((t, WB), BF16),
                   jax.ShapeDtypeStruct((t, WM), BF16)],
        compiler_params=_params("parallel"),
        name="local_attn",
    )(z, z, z, bias, z, mem2d, w_mem_kv)


MXU_COLS = 256


def _gated_merge_kernel(x_ref, oa_ref, ob_ref, om_ref, wg_ref, bg_ref,
                        woa_ref, wob_ref, wom_ref, y_ref):
    d = x_ref.shape[1]
    xb = x_ref[...].astype(BF16)
    branches = ((oa_ref[...], woa_ref), (ob_ref[...], wob_ref), (om_ref[...], wom_ref))
    for c in range(0, d, MXU_COLS):
        cols = slice(c, c + MXU_COLS)
        y = None
        for k, (o, wo_ref) in enumerate(branches):
            gcols = slice(k * d + c, k * d + c + MXU_COLS)
            gate = jax.nn.sigmoid(_dot(xb, wg_ref[:, gcols]) + bg_ref[:, gcols])
            term = gate * _dot(o, wo_ref[:, cols])
            y = term if y is None else y + term
        y_ref[:, cols] = y.astype(BF16)


def _gated_merge(x, oa, ob, om, w_gate, b_gate, w_oa, w_ob, w_om, *, tm=512):
    t, d = x.shape
    assert t % tm == 0 and d % MXU_COLS == 0 and w_gate.shape == (d, N_BRANCHES * d)

    def rows(width):
        return pl.BlockSpec((tm, width), lambda i: (i, 0))

    def resident(arr):
        return pl.BlockSpec(arr.shape, lambda i: (0, 0), pipeline_mode=pl.Buffered(1))

    b_gate = b_gate.reshape(1, N_BRANCHES * d)
    return pl.pallas_call(
        _gated_merge_kernel,
        grid=(t // tm,),
        in_specs=[rows(d), rows(WA_Q), rows(WB), rows(WM),
                  resident(w_gate), resident(b_gate),
                  resident(w_oa), resident(w_ob), resident(w_om)],
        out_specs=rows(d),
        out_shape=jax.ShapeDtypeStruct((t, d), BF16),
        compiler_params=_params("parallel"),
        name="gated_merge",
    )(x, oa, ob, om, w_gate, b_gate, w_oa, w_ob, w_om)


def _outproj_ln_kernel(x_ref, y_ref, w_ref, g_ref, b_ref, o_ref, *, alpha):
    tm = x_ref.shape[0]
    g = g_ref[...]
    b = b_ref[...]
    for r in range(0, tm, LN_OVERLAP_ROWS):
        rows = slice(r, r + LN_OVERLAP_ROWS)
        y = alpha * x_ref[rows, :] + _dot(y_ref[rows, :], w_ref[...])
        o_ref[rows, :] = _layer_norm_rows(y, g, b)


def _outproj_ln(x, y, w_out, ln_g, ln_b, *, alpha, tm=1024):
    t, d = x.shape
    assert t % tm == 0
    vec = pl.BlockSpec((1, d), lambda i: (0, 0))
    return pl.pallas_call(
        functools.partial(_outproj_ln_kernel, alpha=alpha),
        grid=(t // tm,),
        in_specs=[
            pl.BlockSpec((tm, d), lambda i: (i, 0)),
            pl.BlockSpec((tm, d), lambda i: (i, 0)),
            pl.BlockSpec((d, d), lambda i: (0, 0), pipeline_mode=pl.Buffered(1)),
            vec, vec,
        ],
        out_specs=pl.BlockSpec((tm, d), lambda i: (i, 0)),
        out_shape=jax.ShapeDtypeStruct((t, d), F32),
        compiler_params=_params("parallel"),
        name="outproj_ln",
    )(x, y, w_out, ln_g.reshape(1, d), ln_b.reshape(1, d))


def kernel(x, mem, ln1_g, ln1_b, ffn1_w_gu, ffn1_w_down, w_in, b_gate, q_norm_a, k_norm_a,
           na_rpb, w_mem_kv, w_oa, w_ob, w_om, w_out, ln2_g, ln2_b, ffn2_w_gu, ffn2_w_down,
           ln3_g, ln3_b):
    batch, seq, d = x.shape
    mem_len = mem.shape[1]
    depth = w_in.shape[0]
    alpha = (2 * depth) ** 0.25
    h = x.reshape(batch * seq, d)
    mem2d = mem.reshape(batch * mem_len, d)
    for l in range(depth):
        bf = lambda w: w[l].astype(BF16)
        h = _ffn_ln(h, bf(ffn1_w_gu), bf(ffn1_w_down), ln1_g[l], ln1_b[l], alpha=alpha)
        w_attn = w_in[l][:, :W_ATTN_COLS].astype(BF16)
        z, w_gate = _inproj(h, w_attn, w_in[l], q_norm_a[l], k_norm_a[l], seq)
        oa, w_gu2, w_down2, w_oa_b, w_ob_b, w_om_b, w_out_b = _gqa(
            z, batch, seq,
            cast_along=(ffn2_w_gu[l], ffn2_w_down[l], w_oa[l], w_ob[l], w_om[l], w_out[l]))
        ob, om = _local_attn(z, na_rpb[l], mem2d, w_mem_kv[l], batch, seq)
        y = _gated_merge(h, oa, ob, om, w_gate, b_gate[l], w_oa_b, w_ob_b, w_om_b)
        h = _outproj_ln(h, y, w_out_b, ln2_g[l], ln2_b[l], alpha=alpha)
        h = _ffn_ln(h, w_gu2, w_down2, ln3_g[l], ln3_b[l], alpha=alpha)
    return h.reshape(batch, seq, d)
```

```python
import functools

import numpy as np
import jax
import jax.numpy as jnp
from jax import lax
from jax.experimental import pallas as pl
from jax.experimental.pallas import tpu as pltpu

HEAD_DIM = 128
GQA_Q_HEADS = 8
GQA_KV_HEADS = 2
NA_HEADS = 4
MEM_HEADS = 4
GRID_W = 64
NA_ROWS_MAX = 8
NA_COLS = 16
ROPE_THETA = 10000.0
LN_EPS = 1e-5
RMS_EPS = 1e-6
N_BRANCHES = 3
MASK_VALUE = -1e30

WA_Q = GQA_Q_HEADS * HEAD_DIM
WA_KV = GQA_KV_HEADS * HEAD_DIM
WB = NA_HEADS * HEAD_DIM
WM = MEM_HEADS * HEAD_DIM
COL_KA = WA_Q
COL_VA = COL_KA + WA_KV
COL_QB = COL_VA + WA_KV
COL_KB = COL_QB + WB
COL_VB = COL_KB + WB
COL_QM = COL_VB + WB
W_ATTN_COLS = COL_QM + WM

V7X_VMEM_BYTES = 64 * 1024 * 1024
VMEM_LIMIT_BYTES = V7X_VMEM_BYTES - 8 * 1024 * 1024

BF16 = jnp.bfloat16
F32 = jnp.float32


def _params(*semantics):
    return pltpu.CompilerParams(dimension_semantics=semantics,
                                vmem_limit_bytes=VMEM_LIMIT_BYTES)


def _dot(a, b):
    return jnp.dot(a, b, preferred_element_type=F32)


def _dot_nt(a, b):
    return lax.dot_general(a, b, (((1,), (1,)), ((), ())), preferred_element_type=F32)


def _layer_norm_rows(y, g, b):
    mu = jnp.mean(y, axis=-1, keepdims=True)
    yc = y - mu
    var = jnp.mean(yc * yc, axis=-1, keepdims=True)
    return yc * lax.rsqrt(var + LN_EPS) * g + b


LN_ROW_CHUNK = 128


LN_OVERLAP_ROWS = 256


def _for_row_chunks(n_rows, body):
    def step(r, carry):
        body(pl.ds(pl.multiple_of(r * LN_ROW_CHUNK, LN_ROW_CHUNK), LN_ROW_CHUNK))
        return carry
    lax.fori_loop(0, n_rows // LN_ROW_CHUNK, step, 0)


FFN_TM = 1024
FFN_TF = 512
FFN_TF_F32 = 256


def _ffn_ln_kernel(x_ref, wg_ref, wu_ref, wd_ref, g_ref, b_ref, *rest, alpha, has_fill,
                   emit_weights):
    rest = rest[1:] if has_fill else rest
    if emit_weights:
        o_ref, wg_out, wu_out, wd_out, xb_ref = rest
    else:
        o_ref, xb_ref = rest
    j = pl.program_id(1)
    tm = x_ref.shape[0]

    @pl.when(j == 0)
    def _init():
        def body(rows):
            x = x_ref[rows, :]
            xb_ref[rows, :] = x.astype(BF16)
            o_ref[rows, :] = alpha * x
        _for_row_chunks(tm, body)

    wg = wg_ref[...].astype(BF16)
    wu = wu_ref[...].astype(BF16)
    wd = wd_ref[...].astype(BF16)
    if emit_weights:
        wg_out[...] = wg
        wu_out[...] = wu
        wd_out[...] = wd
    xb = xb_ref[...]
    gate = _dot(xb, wg)
    up = _dot(xb, wu)
    h = (gate * jax.nn.sigmoid(gate) * (0.5 * up)).astype(BF16)
    o_ref[...] += _dot(h, wd)

    @pl.when(j == pl.num_programs(1) - 1)
    def _finish():
        g = g_ref[...]
        b = b_ref[...]

        def body(rows):
            o_ref[rows, :] = _layer_norm_rows(o_ref[rows, :], g, b)
        _for_row_chunks(tm, body)


def _ffn_ln(x, w_gate, w_up, up_col0, w_down, ln_g, ln_b, *, alpha, tiles, tf, tm=FFN_TM,
            fill=None, emit_weights=False):
    t, d = x.shape
    f = w_down.shape[0]
    first, count = tiles
    assert (first + count) * tm <= t and f % tf == 0 and up_col0 % tf == 0
    nf = f // tf
    up0 = up_col0 // tf
    row_spec = pl.BlockSpec((tm, d), lambda i, j: (i + first, 0))
    vec = pl.BlockSpec((1, d), lambda i, j: (0, 0))
    gate_spec = pl.BlockSpec((d, tf), lambda i, j: (0, j))
    down_spec = pl.BlockSpec((tf, d), lambda i, j: (j, 0))
    in_specs = [
        (pl.BlockSpec((tm, d), lambda i, j: (i + first, 0), pipeline_mode=pl.Buffered(1))
         if count == 1 else row_spec),
        gate_spec,
        pl.BlockSpec((d, tf), lambda i, j: (0, j + up0)),
        down_spec, vec, vec,
    ]
    args = [x, w_gate, w_up, w_down, ln_g.reshape(1, d), ln_b.reshape(1, d)]
    aliases = {}
    if fill is not None:
        aliases = {len(args): 0}
        in_specs.append(pl.BlockSpec(memory_space=pl.ANY))
        args.append(fill)
    out_specs = [row_spec]
    out_shape = [jax.ShapeDtypeStruct((t, d), F32)]
    if emit_weights:
        out_specs += [gate_spec, gate_spec, down_spec]
        out_shape += [jax.ShapeDtypeStruct((d, f), BF16), jax.ShapeDtypeStruct((d, f), BF16),
                      jax.ShapeDtypeStruct((f, d), BF16)]
    res = pl.pallas_call(
        functools.partial(_ffn_ln_kernel, alpha=alpha, has_fill=fill is not None,
                          emit_weights=emit_weights),
        grid=(count, nf),
        in_specs=in_specs,
        out_specs=out_specs,
        out_shape=out_shape,
        input_output_aliases=aliases,
        scratch_shapes=[pltpu.VMEM((tm, d), BF16)],
        compiler_params=_params("parallel", "arbitrary"),
        name="ffn_ln",
    )(*args)
    return res if emit_weights else res[0]


def _rope_tables(seq):
    t = np.arange(seq)
    pos = np.stack([t // GRID_W, t % GRID_W], axis=1).astype(np.float32)
    lane = np.arange(HEAD_DIM)
    half = lane // (HEAD_DIM // 2)
    within = lane % (HEAD_DIM // 2)
    quarter = HEAD_DIM // 4
    first = within < quarter
    freq = within % quarter
    axis_rot = HEAD_DIM // 2
    inv = jnp.asarray(ROPE_THETA, F32) ** (
        -jnp.arange(0, axis_rot, 2, dtype=F32) / axis_rot)
    ang = jnp.asarray(pos)[:, half] * inv[freq][None, :]
    cos = jnp.cos(ang)
    sin = jnp.sin(ang)
    first = jnp.asarray(first)[None, :]
    sin_up = jnp.where(first, -sin, 0.0)
    sin_dn = jnp.where(first, 0.0, sin)
    return cos, sin_up, sin_dn


def _inproj_kernel(x_ref, w_ref, qg_ref, kg_ref, cos_ref, sup_ref, sdn_ref, wfull_ref,
                   o_ref, wgate_ref):
    wgate_ref[...] = wfull_ref[:, W_ATTN_COLS:].astype(BF16)
    xb = x_ref[...].astype(BF16)
    cos = cos_ref[...]
    sup = sup_ref[...]
    sdn = sdn_ref[...]
    scale = HEAD_DIM ** -0.5
    quarter = HEAD_DIM // 4
    pair = 2 * HEAD_DIM

    def norm_rope(z, gain):
        z = z * lax.rsqrt(jnp.mean(z * z, axis=-1, keepdims=True) + RMS_EPS) * gain
        return (z * cos
                + pltpu.roll(z, HEAD_DIM - quarter, 1) * sup
                + pltpu.roll(z, quarter, 1) * sdn)

    for c in range(0, W_ATTN_COLS, pair):
        z2 = _dot(xb, w_ref[:, c:c + pair])
        for hh in range(2):
            col = c + hh * HEAD_DIM
            z = z2[:, hh * HEAD_DIM:(hh + 1) * HEAD_DIM]
            if col < COL_KA:
                z = norm_rope(z, qg_ref[...]) * scale
            elif col < COL_VA:
                z = norm_rope(z, kg_ref[...])
            elif COL_QB <= col < COL_KB or col >= COL_QM:
                z = z * scale
            o_ref[:, col:col + HEAD_DIM] = z.astype(BF16)


def _inproj(x, w_attn, w_full, q_gain, k_gain, seq, *, tm=512):
    t, d = x.shape
    assert t % tm == 0 and seq % tm == 0
    cos, sup, sdn = _rope_tables(seq)
    nseq = seq // tm
    n_steps = t // tm
    slab = d // n_steps
    assert slab * n_steps == d and slab % 16 == 0
    n_gate = w_full.shape[1] - W_ATTN_COLS
    tab_spec = pl.BlockSpec((tm, HEAD_DIM), lambda i: (i % nseq, 0))
    gain_spec = pl.BlockSpec((1, HEAD_DIM), lambda i: (0, 0))
    return pl.pallas_call(
        _inproj_kernel,
        grid=(n_steps,),
        in_specs=[
            pl.BlockSpec((tm, d), lambda i: (i, 0)),
            pl.BlockSpec((d, W_ATTN_COLS), lambda i: (0, 0), pipeline_mode=pl.Buffered(1)),
            gain_spec, gain_spec, tab_spec, tab_spec, tab_spec,
            pl.BlockSpec((slab, w_full.shape[1]), lambda i: (i, 0)),
        ],
        out_specs=[pl.BlockSpec((tm, W_ATTN_COLS), lambda i: (i, 0)),
                   pl.BlockSpec((slab, n_gate), lambda i: (i, 0))],
        out_shape=[jax.ShapeDtypeStruct((t, W_ATTN_COLS), BF16),
                   jax.ShapeDtypeStruct((d, n_gate), BF16)],
        compiler_params=_params("arbitrary"),
        name="inproj",
    )(x, w_attn, q_gain.reshape(1, HEAD_DIM), k_gain.reshape(1, HEAD_DIM), cos, sup, sdn,
      w_full)


def _with_ones_column(v):
    lane = lax.broadcasted_iota(jnp.int32, v.shape, 1)
    ones_col = jnp.where(lane == 0, 1.0, 0.0).astype(v.dtype)
    return jnp.concatenate([v, ones_col], axis=1)


def _softmax_pv(s, v_ext):
    m = jnp.max(s, axis=-1, keepdims=True)
    p = jnp.exp(s - m).astype(BF16)
    ov = _dot(p, v_ext)
    return ov[:, :HEAD_DIM] / ov[:, HEAD_DIM:HEAD_DIM + 1]


def _softmax_pv_few_keys(s, v):
    m = jnp.max(s, axis=-1, keepdims=True)
    p = jnp.exp(s - m)
    l = jnp.sum(p, axis=-1, keepdims=True)
    return _dot(p.astype(BF16), v) / l


GQA_GROUP = GQA_Q_HEADS // GQA_KV_HEADS
GQA_SUB_ROWS = 256


def _gqa_kernel(q_ref, k_ref, v_ref, *refs):
    n_cast = (len(refs) - 1) // 2
    o_ref = refs[n_cast]
    v_ext = _with_ones_column(v_ref[...])
    for r in range(0, q_ref.shape[0], GQA_SUB_ROWS):
        rows = slice(r, r + GQA_SUB_ROWS)
        for h in range(GQA_GROUP):
            cols = slice(h * HEAD_DIM, (h + 1) * HEAD_DIM)
            s = _dot_nt(q_ref[rows, cols], k_ref[...])
            o_ref[rows, cols] = _softmax_pv(s, v_ext).astype(BF16)
    for src_ref, dst_ref in zip(refs[:n_cast], refs[n_cast + 1:]):
        dst_ref[...] = src_ref[...].astype(BF16)


def _gqa(z, batch, seq, cast_along=(), *, tq=1024):
    t = z.shape[0]
    nq = seq // tq
    gw = GQA_GROUP * HEAD_DIM
    kcol = COL_KA // HEAD_DIM
    vcol = COL_VA // HEAD_DIM
    n_steps = batch * GQA_KV_HEADS * nq

    def slab(arr):
        rows = arr.shape[0] // n_steps
        assert rows * n_steps == arr.shape[0] and rows % 16 == 0
        return pl.BlockSpec((rows, arr.shape[1]),
                            lambda b, g, i: ((b * GQA_KV_HEADS + g) * nq + i, 0))

    slabs = [slab(a) for a in cast_along]
    return pl.pallas_call(
        _gqa_kernel,
        grid=(batch, GQA_KV_HEADS, nq),
        in_specs=[
            pl.BlockSpec((tq, gw), lambda b, g, i: (b * nq + i, g)),
            pl.BlockSpec((seq, HEAD_DIM), lambda b, g, i: (b, kcol + g)),
            pl.BlockSpec((seq, HEAD_DIM), lambda b, g, i: (b, vcol + g)),
        ] + slabs,
        out_specs=[pl.BlockSpec((tq, gw), lambda b, g, i: (b * nq + i, g))] + slabs,
        out_shape=[jax.ShapeDtypeStruct((t, WA_Q), BF16)]
        + [jax.ShapeDtypeStruct(a.shape, BF16) for a in cast_along],
        compiler_params=_params("arbitrary", "arbitrary", "arbitrary"),
        name="gqa",
    )(z, z, z, *cast_along)


NA_Q_ROWS = 4
NA_K_ROWS = NA_Q_ROWS + NA_ROWS_MAX
NA_TQ = NA_Q_ROWS * GRID_W
NA_TK = NA_K_ROWS * GRID_W


def _na_key_row_start(qblock, rows):
    return np.clip(qblock * NA_Q_ROWS - NA_ROWS_MAX // 2, 0, rows - NA_K_ROWS)


def _na_bias_tables(rpb, rows):
    heads, n_dr, n_dc = rpb.shape
    nblocks = rows // NA_Q_ROWS
    kr_win = min(NA_ROWS_MAX, rows)
    period = 2 * GRID_W
    assert period >= GRID_W + NA_COLS and n_dc == 2 * NA_COLS - 1
    v = rpb.astype(F32)
    vpad = jnp.concatenate(
        [v[..., NA_COLS - 1:], jnp.zeros((heads, n_dr, period - n_dc), F32), v[..., :NA_COLS - 1]],
        axis=-1)
    skew = jnp.tile(vpad, (1, 1, GRID_W))[..., :GRID_W * (period - 1)]
    toep = skew.reshape(heads, n_dr, GRID_W, period - 1)[..., :GRID_W]
    qc = np.arange(GRID_W)
    cs = np.clip(qc - NA_COLS // 2, 0, GRID_W - NA_COLS)
    ok_c = (qc[None, :] >= cs[:, None]) & (qc[None, :] < cs[:, None] + NA_COLS)
    toep = jnp.where(jnp.asarray(ok_c)[None, None], toep, MASK_VALUE)
    masked = jnp.full((heads, GRID_W, GRID_W), MASK_VALUE, F32)
    tabs = []
    for qblock in (0, 1, nblocks - 1):
        ks = _na_key_row_start(qblock, rows)
        q_rows = []
        for qi in range(NA_Q_ROWS):
            qr = qblock * NA_Q_ROWS + qi
            rs = np.clip(qr - kr_win // 2, 0, rows - kr_win)
            blocks = []
            for kj in range(NA_K_ROWS):
                kr = ks + kj
                inside = rs <= kr < rs + kr_win
                blocks.append(toep[:, kr - qr + NA_ROWS_MAX - 1] if inside else masked)
            q_rows.append(jnp.concatenate(blocks, axis=-1))
        tabs.append(jnp.concatenate(q_rows, axis=-2))
    return jnp.stack(tabs)


MEM_Q_ROWS = 512


def _local_attn_kernel(qb_ref, kb_ref, vb_ref, bias_ref, qm_ref, mem_ref, wkv_ref,
                       ob_ref, om_ref):
    rows = qb_ref.shape[0] // GRID_W
    n_blocks = rows // NA_Q_ROWS
    for i in range(n_blocks):
        pattern = 0 if i == 0 else (2 if i == n_blocks - 1 else 1)
        qrows = slice(i * NA_TQ, (i + 1) * NA_TQ)
        start = int(_na_key_row_start(i, rows)) * GRID_W
        krows = slice(start, start + NA_TK)
        for h in range(NA_HEADS):
            cols = slice(h * HEAD_DIM, (h + 1) * HEAD_DIM)
            s = _dot_nt(qb_ref[qrows, cols], kb_ref[krows, cols]) + bias_ref[pattern, h]
            v_ext = _with_ones_column(vb_ref[krows, cols])
            ob_ref[qrows, cols] = _softmax_pv(s, v_ext).astype(BF16)
    kvm = _dot(mem_ref[...].astype(BF16), wkv_ref[...].astype(BF16)).astype(BF16)
    for r in range(0, qm_ref.shape[0], MEM_Q_ROWS):
        qrows = slice(r, r + MEM_Q_ROWS)
        for h in range(MEM_HEADS):
            cols = slice(h * HEAD_DIM, (h + 1) * HEAD_DIM)
            s = _dot_nt(qm_ref[qrows, cols], kvm[:, cols])
            v = kvm[:, WM + h * HEAD_DIM:WM + (h + 1) * HEAD_DIM]
            om_ref[qrows, cols] = _softmax_pv_few_keys(s, v).astype(BF16)


def _local_attn(z, rpb, mem2d, w_mem_kv, batch, seq):
    t = z.shape[0]
    mem_len = mem2d.shape[0] // batch
    rows = seq // GRID_W
    assert rows % NA_Q_ROWS == 0 and rows >= NA_K_ROWS and seq % MEM_Q_ROWS == 0
    bias = _na_bias_tables(rpb, rows)

    def per_batch(col_block):
        return pl.BlockSpec((seq, WB), lambda b: (b, col_block))

    def resident(arr):
        return pl.BlockSpec(arr.shape, lambda b: (0,) * arr.ndim, pipeline_mode=pl.Buffered(1))

    return pl.pallas_call(
        _local_attn_kernel,
        grid=(batch,),
        in_specs=[
            per_batch(COL_QB // WB), per_batch(COL_KB // WB), per_batch(COL_VB // WB),
            resident(bias),
            pl.BlockSpec((seq, WM), lambda b: (b, COL_QM // WM)),
            pl.BlockSpec((mem_len, mem2d.shape[1]), lambda b: (b, 0)),
            resident(w_mem_kv),
        ],
        out_specs=[pl.BlockSpec((seq, WB), lambda b: (b, 0)),
                   pl.BlockSpec((seq, WM), lambda b: (b, 0))],
        out_shape=[jax.ShapeDtypeStruct((t, WB), BF16),
                   jax.ShapeDtypeStruct((t, WM), BF16)],
        compiler_params=_params("parallel"),
        name="local_attn",
    )(z, z, z, bias, z, mem2d, w_mem_kv)


MXU_COLS = 256


def _gated_merge_kernel(x_ref, oa_ref, ob_ref, om_ref, wg_ref, bg_ref,
                        woa_ref, wob_ref, wom_ref, y_ref):
    d = x_ref.shape[1]
    xb = x_ref[...].astype(BF16)
    branches = ((oa_ref[...], woa_ref), (ob_ref[...], wob_ref), (om_ref[...], wom_ref))
    for c in range(0, d, MXU_COLS):
        cols = slice(c, c + MXU_COLS)
        y = None
        for k, (o, wo_ref) in enumerate(branches):
            gcols = slice(k * d + c, k * d + c + MXU_COLS)
            gate = jax.nn.sigmoid(_dot(xb, wg_ref[:, gcols]) + bg_ref[:, gcols])
            term = gate * _dot(o, wo_ref[:, cols])
            y = term if y is None else y + term
        y_ref[:, cols] = y.astype(BF16)


def _gated_merge(x, oa, ob, om, w_gate, b_gate, w_oa, w_ob, w_om, *, tm=512):
    t, d = x.shape
    assert t % tm == 0 and d % MXU_COLS == 0 and w_gate.shape == (d, N_BRANCHES * d)

    def rows(width):
        return pl.BlockSpec((tm, width), lambda i: (i, 0))

    def resident(arr):
        return pl.BlockSpec(arr.shape, lambda i: (0, 0), pipeline_mode=pl.Buffered(1))

    b_gate = b_gate.reshape(1, N_BRANCHES * d)
    return pl.pallas_call(
        _gated_merge_kernel,
        grid=(t // tm,),
        in_specs=[rows(d), rows(WA_Q), rows(WB), rows(WM),
                  resident(w_gate), resident(b_gate),
                  resident(w_oa), resident(w_ob), resident(w_om)],
        out_specs=rows(d),
        out_shape=jax.ShapeDtypeStruct((t, d), BF16),
        compiler_params=_params("parallel"),
        name="gated_merge",
    )(x, oa, ob, om, w_gate, b_gate, w_oa, w_ob, w_om)


def _outproj_ln_kernel(x_ref, y_ref, w_ref, g_ref, b_ref, o_ref, *, alpha):
    tm = x_ref.shape[0]
    g = g_ref[...]
    b = b_ref[...]
    for r in range(0, tm, LN_OVERLAP_ROWS):
        rows = slice(r, r + LN_OVERLAP_ROWS)
        y = alpha * x_ref[rows, :] + _dot(y_ref[rows, :], w_ref[...])
        o_ref[rows, :] = _layer_norm_rows(y, g, b)


def _outproj_ln(x, y, w_out, ln_g, ln_b, *, alpha, tm=1024):
    t, d = x.shape
    assert t % tm == 0
    vec = pl.BlockSpec((1, d), lambda i: (0, 0))
    return pl.pallas_call(
        functools.partial(_outproj_ln_kernel, alpha=alpha),
        grid=(t // tm,),
        in_specs=[
            pl.BlockSpec((tm, d), lambda i: (i, 0)),
            pl.BlockSpec((tm, d), lambda i: (i, 0)),
            pl.BlockSpec((d, d), lambda i: (0, 0), pipeline_mode=pl.Buffered(1)),
            vec, vec,
        ],
        out_specs=pl.BlockSpec((tm, d), lambda i: (i, 0)),
        out_shape=jax.ShapeDtypeStruct((t, d), F32),
        compiler_params=_params("parallel"),
        name="outproj_ln",
    )(x, y, w_out, ln_g.reshape(1, d), ln_b.reshape(1, d))


def kernel(x, mem, ln1_g, ln1_b, ffn1_w_gu, ffn1_w_down, w_in, b_gate, q_norm_a, k_norm_a,
           na_rpb, w_mem_kv, w_oa, w_ob, w_om, w_out, ln2_g, ln2_b, ffn2_w_gu, ffn2_w_down,
           ln3_g, ln3_b):
    batch, seq, d = x.shape
    mem_len = mem.shape[1]
    depth = w_in.shape[0]
    alpha = (2 * depth) ** 0.25
    h = x.reshape(batch * seq, d)
    mem2d = mem.reshape(batch * mem_len, d)
    d_ff = ffn1_w_down.shape[1]
    n_tiles = batch * seq // FFN_TM
    for l in range(depth):
        ffn1 = functools.partial(_ffn_ln, ln_g=ln1_g[l], ln_b=ln1_b[l], alpha=alpha)
        head, w_gate1, w_up1, w_down1 = ffn1(
            h, ffn1_w_gu[l], ffn1_w_gu[l], d_ff, ffn1_w_down[l],
            tiles=(0, 1), tf=FFN_TF_F32, emit_weights=True)
        h = ffn1(h, w_gate1, w_up1, 0, w_down1, tiles=(1, n_tiles - 1), tf=FFN_TF, fill=head)
        w_attn = w_in[l][:, :W_ATTN_COLS].astype(BF16)
        z, w_gate = _inproj(h, w_attn, w_in[l], q_norm_a[l], k_norm_a[l], seq)
        oa, w_gu2, w_down2, w_oa_b, w_ob_b, w_om_b, w_out_b = _gqa(
            z, batch, seq,
            cast_along=(ffn2_w_gu[l], ffn2_w_down[l], w_oa[l], w_ob[l], w_om[l], w_out[l]))
        ob, om = _local_attn(z, na_rpb[l], mem2d, w_mem_kv[l], batch, seq)
        y = _gated_merge(h, oa, ob, om, w_gate, b_gate[l], w_oa_b, w_ob_b, w_om_b)
        h = _outproj_ln(h, y, w_out_b, ln2_g[l], ln2_b[l], alpha=alpha)
        h = _ffn_ln(h, w_gu2, w_gu2, d_ff, w_down2, ln3_g[l], ln3_b[l], alpha=alpha,
                    tiles=(0, n_tiles), tf=FFN_TF)
    return h.reshape(batch, seq, d)
```

```python
import functools

import numpy as np
import jax
import jax.numpy as jnp
from jax import lax
from jax.experimental import pallas as pl
from jax.experimental.pallas import tpu as pltpu

HEAD_DIM = 128
GQA_Q_HEADS = 8
GQA_KV_HEADS = 2
NA_HEADS = 4
MEM_HEADS = 4
GRID_W = 64
NA_ROWS_MAX = 8
NA_COLS = 16
ROPE_THETA = 10000.0
LN_EPS = 1e-5
RMS_EPS = 1e-6
N_BRANCHES = 3
MASK_VALUE = -1e30

WA_Q = GQA_Q_HEADS * HEAD_DIM
WA_KV = GQA_KV_HEADS * HEAD_DIM
WB = NA_HEADS * HEAD_DIM
WM = MEM_HEADS * HEAD_DIM
COL_KA = WA_Q
COL_VA = COL_KA + WA_KV
COL_QB = COL_VA + WA_KV
COL_KB = COL_QB + WB
COL_VB = COL_KB + WB
COL_QM = COL_VB + WB
W_ATTN_COLS = COL_QM + WM

V7X_VMEM_BYTES = 64 * 1024 * 1024
VMEM_LIMIT_BYTES = V7X_VMEM_BYTES - 8 * 1024 * 1024

BF16 = jnp.bfloat16
F32 = jnp.float32
BF16_ROW_TILE = 16


def _params(*semantics):
    return pltpu.CompilerParams(dimension_semantics=semantics,
                                vmem_limit_bytes=VMEM_LIMIT_BYTES)


def _dot(a, b):
    return jnp.dot(a, b, preferred_element_type=F32)


def _dot_nt(a, b):
    return lax.dot_general(a, b, (((1,), (1,)), ((), ())), preferred_element_type=F32)


def _layer_norm_rows(y, g, b):
    mu = jnp.mean(y, axis=-1, keepdims=True)
    yc = y - mu
    var = jnp.mean(yc * yc, axis=-1, keepdims=True)
    return yc * lax.rsqrt(var + LN_EPS) * g + b


LN_ROW_CHUNK = 128


LN_OVERLAP_ROWS = 256


def _for_row_chunks(n_rows, body):
    def step(r, carry):
        body(pl.ds(pl.multiple_of(r * LN_ROW_CHUNK, LN_ROW_CHUNK), LN_ROW_CHUNK))
        return carry
    lax.fori_loop(0, n_rows // LN_ROW_CHUNK, step, 0)


FFN_TM = 1024
FFN_TF = 512
FFN_TF_F32 = 256


def _when_all(*conds):
    traced = [c for c in conds if c is not True]
    if not traced:
        return lambda f: f()
    cond = traced[0]
    for c in traced[1:]:
        cond = jnp.logical_and(cond, c)
    return pl.when(cond)


def _ffn_ln_kernel(x_ref, wg_ref, wu_ref, wd_ref, g_ref, b_ref, *rest, alpha, n_done,
                   emit_weights):
    if n_done:
        done_ref, o_ref, xb_ref, done_sem = rest
    elif emit_weights:
        o_ref, wg_out, wu_out, wd_out, xb_ref = rest
    else:
        o_ref, xb_ref = rest
    i = pl.program_id(0)
    j = pl.program_id(1)
    tm = x_ref.shape[0]
    live = True
    if n_done:
        live = i >= n_done

        @pl.when(jnp.logical_and(i < n_done, j == 0))
        def _take_done_tile():
            rows = pl.ds(pl.multiple_of(i * tm, tm), tm)
            copy = pltpu.make_async_copy(done_ref.at[rows, :], o_ref, done_sem)
            copy.start()
            copy.wait()

    @_when_all(live, j == 0)
    def _init():
        def body(rows):
            x = x_ref[rows, :]
            xb_ref[rows, :] = x.astype(BF16)
            o_ref[rows, :] = alpha * x
        _for_row_chunks(tm, body)

    @_when_all(live)
    def _accumulate():
        wg = wg_ref[...].astype(BF16)
        wu = wu_ref[...].astype(BF16)
        wd = wd_ref[...].astype(BF16)
        if emit_weights:
            wg_out[...] = wg
            wu_out[...] = wu
            wd_out[...] = wd
        xb = xb_ref[...]
        gate = _dot(xb, wg)
        up = _dot(xb, wu)
        h = (gate * jax.nn.sigmoid(gate) * (0.5 * up)).astype(BF16)
        o_ref[...] += _dot(h, wd)

    @_when_all(live, j == pl.num_programs(1) - 1)
    def _finish():
        g = g_ref[...]
        b = b_ref[...]

        def body(rows):
            o_ref[rows, :] = _layer_norm_rows(o_ref[rows, :], g, b)
        _for_row_chunks(tm, body)


def _ffn_ln(x, w_gate, w_up, up_col0, w_down, ln_g, ln_b, *, alpha, n_tiles, tf, tm=FFN_TM,
            done=None, emit_weights=False):
    t, d = x.shape
    f = w_down.shape[0]
    assert n_tiles * tm <= t and f % tf == 0 and up_col0 % tf == 0
    n_done = 0 if done is None else done.shape[0] // tm
    assert not (n_done and emit_weights) and (done is None or done.shape == (n_done * tm, d))
    nf = f // tf
    up0 = up_col0 // tf

    def chunk(i, j):
        return jnp.where(i >= n_done, j, 0) if n_done else j

    row_spec = pl.BlockSpec((tm, d), lambda i, j: (i, 0))
    vec = pl.BlockSpec((1, d), lambda i, j: (0, 0))
    gate_spec = pl.BlockSpec((d, tf), lambda i, j: (0, chunk(i, j)))
    down_spec = pl.BlockSpec((tf, d), lambda i, j: (chunk(i, j), 0))
    x_index = lambda i, j: (jnp.maximum(i, n_done), 0)
    in_specs = [
        pl.BlockSpec((tm, d), x_index, pipeline_mode=pl.Buffered(1)) if n_tiles == 1
        else pl.BlockSpec((tm, d), x_index),
        gate_spec,
        pl.BlockSpec((d, tf), lambda i, j: (0, chunk(i, j) + up0)),
        down_spec, vec, vec,
    ]
    args = [x, w_gate, w_up, w_down, ln_g.reshape(1, d), ln_b.reshape(1, d)]
    scratch = [pltpu.VMEM((tm, d), BF16)]
    if n_done:
        in_specs.append(pl.BlockSpec(memory_space=pl.ANY))
        args.append(done)
        scratch.append(pltpu.SemaphoreType.DMA(()))
    out_specs = [row_spec]
    out_shape = [jax.ShapeDtypeStruct((n_tiles * tm, d), F32)]
    if emit_weights:
        out_specs += [gate_spec, gate_spec, down_spec]
        out_shape += [jax.ShapeDtypeStruct((d, f), BF16), jax.ShapeDtypeStruct((d, f), BF16),
                      jax.ShapeDtypeStruct((f, d), BF16)]
    res = pl.pallas_call(
        functools.partial(_ffn_ln_kernel, alpha=alpha, n_done=n_done,
                          emit_weights=emit_weights),
        grid=(n_tiles, nf),
        in_specs=in_specs,
        out_specs=out_specs,
        out_shape=out_shape,
        scratch_shapes=scratch,
        compiler_params=_params("arbitrary", "arbitrary"),
        name="ffn_ln",
    )(*args)
    return res if emit_weights else res[0]


def _rope_tables(seq):
    t = np.arange(seq)
    pos = np.stack([t // GRID_W, t % GRID_W], axis=1).astype(np.float32)
    lane = np.arange(HEAD_DIM)
    half = lane // (HEAD_DIM // 2)
    within = lane % (HEAD_DIM // 2)
    quarter = HEAD_DIM // 4
    first = within < quarter
    freq = within % quarter
    axis_rot = HEAD_DIM // 2
    inv = jnp.asarray(ROPE_THETA, F32) ** (
        -jnp.arange(0, axis_rot, 2, dtype=F32) / axis_rot)
    ang = jnp.asarray(pos)[:, half] * inv[freq][None, :]
    cos = jnp.cos(ang)
    sin = jnp.sin(ang)
    first = jnp.asarray(first)[None, :]
    sin_up = jnp.where(first, -sin, 0.0)
    sin_dn = jnp.where(first, 0.0, sin)
    return cos, sin_up, sin_dn


def _inproj_kernel(x_ref, w_ref, qg_ref, kg_ref, cos_ref, sup_ref, sdn_ref, wfull_ref,
                   o_ref, wgate_ref):
    wgate_ref[...] = wfull_ref[:, W_ATTN_COLS:].astype(BF16)
    xb = x_ref[...].astype(BF16)
    cos = cos_ref[...]
    sup = sup_ref[...]
    sdn = sdn_ref[...]
    scale = HEAD_DIM ** -0.5
    quarter = HEAD_DIM // 4
    pair = 2 * HEAD_DIM

    def norm_rope(z, gain):
        z = z * lax.rsqrt(jnp.mean(z * z, axis=-1, keepdims=True) + RMS_EPS) * gain
        return (z * cos
                + pltpu.roll(z, HEAD_DIM - quarter, 1) * sup
                + pltpu.roll(z, quarter, 1) * sdn)

    for c in range(0, W_ATTN_COLS, pair):
        z2 = _dot(xb, w_ref[:, c:c + pair])
        for hh in range(2):
            col = c + hh * HEAD_DIM
            z = z2[:, hh * HEAD_DIM:(hh + 1) * HEAD_DIM]
            if col < COL_KA:
                z = norm_rope(z, qg_ref[...]) * scale
            elif col < COL_VA:
                z = norm_rope(z, kg_ref[...])
            elif COL_QB <= col < COL_KB or col >= COL_QM:
                z = z * scale
            o_ref[:, col:col + HEAD_DIM] = z.astype(BF16)


def _inproj(x, w_attn, w_full, q_gain, k_gain, seq, *, tm=512):
    t, d = x.shape
    assert t % tm == 0 and seq % tm == 0
    cos, sup, sdn = _rope_tables(seq)
    nseq = seq // tm
    n_steps = t // tm
    slab = d // n_steps
    assert slab * n_steps == d and slab % BF16_ROW_TILE == 0
    n_gate = w_full.shape[1] - W_ATTN_COLS
    tab_spec = pl.BlockSpec((tm, HEAD_DIM), lambda i: (i % nseq, 0))
    gain_spec = pl.BlockSpec((1, HEAD_DIM), lambda i: (0, 0))
    return pl.pallas_call(
        _inproj_kernel,
        grid=(n_steps,),
        in_specs=[
            pl.BlockSpec((tm, d), lambda i: (i, 0)),
            pl.BlockSpec((d, W_ATTN_COLS), lambda i: (0, 0), pipeline_mode=pl.Buffered(1)),
            gain_spec, gain_spec, tab_spec, tab_spec, tab_spec,
            pl.BlockSpec((slab, w_full.shape[1]), lambda i: (i, 0)),
        ],
        out_specs=[pl.BlockSpec((tm, W_ATTN_COLS), lambda i: (i, 0)),
                   pl.BlockSpec((slab, n_gate), lambda i: (i, 0))],
        out_shape=[jax.ShapeDtypeStruct((t, W_ATTN_COLS), BF16),
                   jax.ShapeDtypeStruct((d, n_gate), BF16)],
        compiler_params=_params("arbitrary"),
        name="inproj",
    )(x, w_attn, q_gain.reshape(1, HEAD_DIM), k_gain.reshape(1, HEAD_DIM), cos, sup, sdn,
      w_full)


def _with_ones_column(v):
    lane = lax.broadcasted_iota(jnp.int32, v.shape, 1)
    ones_col = jnp.where(lane == 0, 1.0, 0.0).astype(v.dtype)
    return jnp.concatenate([v, ones_col], axis=1)


def _softmax_pv(s, v_ext):
    m = jnp.max(s, axis=-1, keepdims=True)
    p = jnp.exp(s - m).astype(BF16)
    ov = _dot(p, v_ext)
    return ov[:, :HEAD_DIM] / ov[:, HEAD_DIM:HEAD_DIM + 1]


def _softmax_pv_few_keys(s, v):
    m = jnp.max(s, axis=-1, keepdims=True)
    p = jnp.exp(s - m)
    l = jnp.sum(p, axis=-1, keepdims=True)
    return _dot(p.astype(BF16), v) / l


GQA_GROUP = GQA_Q_HEADS // GQA_KV_HEADS
GQA_SUB_ROWS = 256


def _gqa_kernel(q_ref, k_ref, v_ref, *refs):
    n_cast = (len(refs) - 1) // 2
    o_ref = refs[n_cast]
    v_ext = _with_ones_column(v_ref[...])
    for r in range(0, q_ref.shape[0], GQA_SUB_ROWS):
        rows = slice(r, r + GQA_SUB_ROWS)
        for h in range(GQA_GROUP):
            cols = slice(h * HEAD_DIM, (h + 1) * HEAD_DIM)
            s = _dot_nt(q_ref[rows, cols], k_ref[...])
            o_ref[rows, cols] = _softmax_pv(s, v_ext).astype(BF16)
    for src_ref, dst_ref in zip(refs[:n_cast], refs[n_cast + 1:]):
        dst_ref[...] = src_ref[...].astype(BF16)


def _gqa(z, batch, seq, cast_along=(), *, tq=1024):
    t = z.shape[0]
    nq = seq // tq
    gw = GQA_GROUP * HEAD_DIM
    kcol = COL_KA // HEAD_DIM
    vcol = COL_VA // HEAD_DIM
    n_steps = batch * GQA_KV_HEADS * nq

    def slab(arr):
        rows = arr.shape[0] // n_steps
        assert rows * n_steps == arr.shape[0] and rows % BF16_ROW_TILE == 0
        return pl.BlockSpec((rows, arr.shape[1]),
                            lambda b, g, i: ((b * GQA_KV_HEADS + g) * nq + i, 0))

    slabs = [slab(a) for a in cast_along]
    return pl.pallas_call(
        _gqa_kernel,
        grid=(batch, GQA_KV_HEADS, nq),
        in_specs=[
            pl.BlockSpec((tq, gw), lambda b, g, i: (b * nq + i, g)),
            pl.BlockSpec((seq, HEAD_DIM), lambda b, g, i: (b, kcol + g)),
            pl.BlockSpec((seq, HEAD_DIM), lambda b, g, i: (b, vcol + g)),
        ] + slabs,
        out_specs=[pl.BlockSpec((tq, gw), lambda b, g, i: (b * nq + i, g))] + slabs,
        out_shape=[jax.ShapeDtypeStruct((t, WA_Q), BF16)]
        + [jax.ShapeDtypeStruct(a.shape, BF16) for a in cast_along],
        compiler_params=_params("arbitrary", "arbitrary", "arbitrary"),
        name="gqa",
    )(z, z, z, *cast_along)


NA_Q_ROWS = 4
NA_K_ROWS = NA_Q_ROWS + NA_ROWS_MAX
NA_TQ = NA_Q_ROWS * GRID_W
NA_TK = NA_K_ROWS * GRID_W


def _na_key_row_start(qblock, rows):
    return np.clip(qblock * NA_Q_ROWS - NA_ROWS_MAX // 2, 0, rows - NA_K_ROWS)


def _na_bias_tables(rpb, rows):
    heads, n_dr, n_dc = rpb.shape
    nblocks = rows // NA_Q_ROWS
    kr_win = min(NA_ROWS_MAX, rows)
    period = 2 * GRID_W
    assert period >= GRID_W + NA_COLS and n_dc == 2 * NA_COLS - 1
    v = rpb.astype(F32)
    vpad = jnp.concatenate(
        [v[..., NA_COLS - 1:], jnp.zeros((heads, n_dr, period - n_dc), F32), v[..., :NA_COLS - 1]],
        axis=-1)
    skew = jnp.tile(vpad, (1, 1, GRID_W))[..., :GRID_W * (period - 1)]
    toep = skew.reshape(heads, n_dr, GRID_W, period - 1)[..., :GRID_W]
    qc = np.arange(GRID_W)
    cs = np.clip(qc - NA_COLS // 2, 0, GRID_W - NA_COLS)
    ok_c = (qc[None, :] >= cs[:, None]) & (qc[None, :] < cs[:, None] + NA_COLS)
    toep = jnp.where(jnp.asarray(ok_c)[None, None], toep, MASK_VALUE)
    masked = jnp.full((heads, GRID_W, GRID_W), MASK_VALUE, F32)
    tabs = []
    for qblock in (0, 1, nblocks - 1):
        ks = _na_key_row_start(qblock, rows)
        q_rows = []
        for qi in range(NA_Q_ROWS):
            qr = qblock * NA_Q_ROWS + qi
            rs = np.clip(qr - kr_win // 2, 0, rows - kr_win)
            blocks = []
            for kj in range(NA_K_ROWS):
                kr = ks + kj
                inside = rs <= kr < rs + kr_win
                blocks.append(toep[:, kr - qr + NA_ROWS_MAX - 1] if inside else masked)
            q_rows.append(jnp.concatenate(blocks, axis=-1))
        tabs.append(jnp.concatenate(q_rows, axis=-2))
    return jnp.stack(tabs)


MEM_Q_ROWS = 512


def _local_attn_kernel(qb_ref, kb_ref, vb_ref, bias_ref, qm_ref, mem_ref, wkv_ref,
                       ob_ref, om_ref):
    rows = qb_ref.shape[0] // GRID_W
    n_blocks = rows // NA_Q_ROWS
    for i in range(n_blocks):
        pattern = 0 if i == 0 else (2 if i == n_blocks - 1 else 1)
        qrows = slice(i * NA_TQ, (i + 1) * NA_TQ)
        start = int(_na_key_row_start(i, rows)) * GRID_W
        krows = slice(start, start + NA_TK)
        for h in range(NA_HEADS):
            cols = slice(h * HEAD_DIM, (h + 1) * HEAD_DIM)
            s = _dot_nt(qb_ref[qrows, cols], kb_ref[krows, cols]) + bias_ref[pattern, h]
            v_ext = _with_ones_column(vb_ref[krows, cols])
            ob_ref[qrows, cols] = _softmax_pv(s, v_ext).astype(BF16)
    kvm = _dot(mem_ref[...].astype(BF16), wkv_ref[...].astype(BF16)).astype(BF16)
    for r in range(0, qm_ref.shape[0], MEM_Q_ROWS):
        qrows = slice(r, r + MEM_Q_ROWS)
        for h in range(MEM_HEADS):
            cols = slice(h * HEAD_DIM, (h + 1) * HEAD_DIM)
            s = _dot_nt(qm_ref[qrows, cols], kvm[:, cols])
            v = kvm[:, WM + h * HEAD_DIM:WM + (h + 1) * HEAD_DIM]
            om_ref[qrows, cols] = _softmax_pv_few_keys(s, v).astype(BF16)


def _local_attn(z, rpb, mem2d, w_mem_kv, batch, seq):
    t = z.shape[0]
    mem_len = mem2d.shape[0] // batch
    rows = seq // GRID_W
    assert rows % NA_Q_ROWS == 0 and rows >= NA_K_ROWS and seq % MEM_Q_ROWS == 0
    bias = _na_bias_tables(rpb, rows)

    def per_batch(col_block):
        return pl.BlockSpec((seq, WB), lambda b: (b, col_block))

    def resident(arr):
        return pl.BlockSpec(arr.shape, lambda b: (0,) * arr.ndim, pipeline_mode=pl.Buffered(1))

    return pl.pallas_call(
        _local_attn_kernel,
        grid=(batch,),
        in_specs=[
            per_batch(COL_QB // WB), per_batch(COL_KB // WB), per_batch(COL_VB // WB),
            resident(bias),
            pl.BlockSpec((seq, WM), lambda b: (b, COL_QM // WM)),
            pl.BlockSpec((mem_len, mem2d.shape[1]), lambda b: (b, 0)),
            resident(w_mem_kv),
        ],
        out_specs=[pl.BlockSpec((seq, WB), lambda b: (b, 0)),
                   pl.BlockSpec((seq, WM), lambda b: (b, 0))],
        out_shape=[jax.ShapeDtypeStruct((t, WB), BF16),
                   jax.ShapeDtypeStruct((t, WM), BF16)],
        compiler_params=_params("parallel"),
        name="local_attn",
    )(z, z, z, bias, z, mem2d, w_mem_kv)


MXU_COLS = 256


def _gated_merge_kernel(x_ref, oa_ref, ob_ref, om_ref, wg_ref, bg_ref,
                        woa_ref, wob_ref, wom_ref, y_ref):
    d = x_ref.shape[1]
    xb = x_ref[...].astype(BF16)
    branches = ((oa_ref[...], woa_ref), (ob_ref[...], wob_ref), (om_ref[...], wom_ref))
    for c in range(0, d, MXU_COLS):
        cols = slice(c, c + MXU_COLS)
        y = None
        for k, (o, wo_ref) in enumerate(branches):
            gcols = slice(k * d + c, k * d + c + MXU_COLS)
            gate = jax.nn.sigmoid(_dot(xb, wg_ref[:, gcols]) + bg_ref[:, gcols])
            term = gate * _dot(o, wo_ref[:, cols])
            y = term if y is None else y + term
        y_ref[:, cols] = y.astype(BF16)


def _gated_merge(x, oa, ob, om, w_gate, b_gate, w_oa, w_ob, w_om, *, tm=512):
    t, d = x.shape
    assert t % tm == 0 and d % MXU_COLS == 0 and w_gate.shape == (d, N_BRANCHES * d)

    def rows(width):
        return pl.BlockSpec((tm, width), lambda i: (i, 0))

    def resident(arr):
        return pl.BlockSpec(arr.shape, lambda i: (0, 0), pipeline_mode=pl.Buffered(1))

    b_gate = b_gate.reshape(1, N_BRANCHES * d)
    return pl.pallas_call(
        _gated_merge_kernel,
        grid=(t // tm,),
        in_specs=[rows(d), rows(WA_Q), rows(WB), rows(WM),
                  resident(w_gate), resident(b_gate),
                  resident(w_oa), resident(w_ob), resident(w_om)],
        out_specs=rows(d),
        out_shape=jax.ShapeDtypeStruct((t, d), BF16),
        compiler_params=_params("parallel"),
        name="gated_merge",
    )(x, oa, ob, om, w_gate, b_gate, w_oa, w_ob, w_om)


def _outproj_ln_kernel(x_ref, y_ref, w_ref, g_ref, b_ref, o_ref, *, alpha):
    tm = x_ref.shape[0]
    g = g_ref[...]
    b = b_ref[...]
    for r in range(0, tm, LN_OVERLAP_ROWS):
        rows = slice(r, r + LN_OVERLAP_ROWS)
        y = alpha * x_ref[rows, :] + _dot(y_ref[rows, :], w_ref[...])
        o_ref[rows, :] = _layer_norm_rows(y, g, b)


def _outproj_ln(x, y, w_out, ln_g, ln_b, *, alpha, tm=1024):
    t, d = x.shape
    assert t % tm == 0
    vec = pl.BlockSpec((1, d), lambda i: (0, 0))
    return pl.pallas_call(
        functools.partial(_outproj_ln_kernel, alpha=alpha),
        grid=(t // tm,),
        in_specs=[
            pl.BlockSpec((tm, d), lambda i: (i, 0)),
            pl.BlockSpec((tm, d), lambda i: (i, 0)),
            pl.BlockSpec((d, d), lambda i: (0, 0), pipeline_mode=pl.Buffered(1)),
            vec, vec,
        ],
        out_specs=pl.BlockSpec((tm, d), lambda i: (i, 0)),
        out_shape=jax.ShapeDtypeStruct((t, d), F32),
        compiler_params=_params("parallel"),
        name="outproj_ln",
    )(x, y, w_out, ln_g.reshape(1, d), ln_b.reshape(1, d))


def kernel(x, mem, ln1_g, ln1_b, ffn1_w_gu, ffn1_w_down, w_in, b_gate, q_norm_a, k_norm_a,
           na_rpb, w_mem_kv, w_oa, w_ob, w_om, w_out, ln2_g, ln2_b, ffn2_w_gu, ffn2_w_down,
           ln3_g, ln3_b):
    batch, seq, d = x.shape
    mem_len = mem.shape[1]
    depth = w_in.shape[0]
    alpha = (2 * depth) ** 0.25
    h = x.reshape(batch * seq, d)
    mem2d = mem.reshape(batch * mem_len, d)
    d_ff = ffn1_w_down.shape[1]
    n_tiles = batch * seq // FFN_TM
    for l in range(depth):
        ffn1 = functools.partial(_ffn_ln, ln_g=ln1_g[l], ln_b=ln1_b[l], alpha=alpha)
        head, w_gate1, w_up1, w_down1 = ffn1(
            h, ffn1_w_gu[l], ffn1_w_gu[l], d_ff, ffn1_w_down[l],
            n_tiles=1, tf=FFN_TF_F32, emit_weights=True)
        h = ffn1(h, w_gate1, w_up1, 0, w_down1, n_tiles=n_tiles, tf=FFN_TF, done=head)
        w_attn = w_in[l][:, :W_ATTN_COLS].astype(BF16)
        z, w_gate = _inproj(h, w_attn, w_in[l], q_norm_a[l], k_norm_a[l], seq)
        oa, w_gu2, w_down2, w_oa_b, w_ob_b, w_om_b, w_out_b = _gqa(
            z, batch, seq,
            cast_along=(ffn2_w_gu[l], ffn2_w_down[l], w_oa[l], w_ob[l], w_om[l], w_out[l]))
        ob, om = _local_attn(z, na_rpb[l], mem2d, w_mem_kv[l], batch, seq)
        y = _gated_merge(h, oa, ob, om, w_gate, b_gate[l], w_oa_b, w_ob_b, w_om_b)
        h = _outproj_ln(h, y, w_out_b, ln2_g[l], ln2_b[l], alpha=alpha)
        h = _ffn_ln(h, w_gu2, w_gu2, d_ff, w_down2, ln3_g[l], ln3_b[l], alpha=alpha,
                    n_tiles=n_tiles, tf=FFN_TF)
    return h.reshape(batch, seq, d)
```

```python
import functools

import numpy as np
import jax
import jax.numpy as jnp
from jax import lax
from jax.experimental import pallas as pl
from jax.experimental.pallas import tpu as pltpu

HEAD_DIM = 128
GQA_Q_HEADS = 8
GQA_KV_HEADS = 2
NA_HEADS = 4
MEM_HEADS = 4
GRID_W = 64
NA_ROWS_MAX = 8
NA_COLS = 16
ROPE_THETA = 10000.0
LN_EPS = 1e-5
RMS_EPS = 1e-6
N_BRANCHES = 3
MASK_VALUE = -1e30

WA_Q = GQA_Q_HEADS * HEAD_DIM
WA_KV = GQA_KV_HEADS * HEAD_DIM
WB = NA_HEADS * HEAD_DIM
WM = MEM_HEADS * HEAD_DIM
COL_KA = WA_Q
COL_VA = COL_KA + WA_KV
COL_QB = COL_VA + WA_KV
COL_KB = COL_QB + WB
COL_VB = COL_KB + WB
COL_QM = COL_VB + WB
W_ATTN_COLS = COL_QM + WM

V7X_VMEM_BYTES = 64 * 1024 * 1024
VMEM_LIMIT_BYTES = V7X_VMEM_BYTES - 8 * 1024 * 1024

BF16 = jnp.bfloat16
F32 = jnp.float32
BF16_ROW_TILE = 16


def _params(*semantics):
    return pltpu.CompilerParams(dimension_semantics=semantics,
                                vmem_limit_bytes=VMEM_LIMIT_BYTES)


def _dot(a, b):
    return jnp.dot(a, b, preferred_element_type=F32)


def _dot_nt(a, b):
    return lax.dot_general(a, b, (((1,), (1,)), ((), ())), preferred_element_type=F32)


def _layer_norm_rows(y, g, b):
    mu = jnp.mean(y, axis=-1, keepdims=True)
    yc = y - mu
    var = jnp.mean(yc * yc, axis=-1, keepdims=True)
    return yc * lax.rsqrt(var + LN_EPS) * g + b


LN_ROW_CHUNK = 256


LN_OVERLAP_ROWS = 256


def _for_row_chunks(n_rows, body):
    def step(r, carry):
        body(pl.ds(pl.multiple_of(r * LN_ROW_CHUNK, LN_ROW_CHUNK), LN_ROW_CHUNK))
        return carry
    lax.fori_loop(0, n_rows // LN_ROW_CHUNK, step, 0)


FFN_TM = 1024
FFN_TF = 512
FFN_TF_F32 = 256


def _when_all(*conds):
    traced = [c for c in conds if c is not True]
    if not traced:
        return lambda f: f()
    cond = traced[0]
    for c in traced[1:]:
        cond = jnp.logical_and(cond, c)
    return pl.when(cond)


def _ffn_ln_kernel(x_ref, wg_ref, wu_ref, wd_ref, g_ref, b_ref, *rest, alpha, n_done,
                   emit_weights):
    if n_done:
        done_ref, o_ref, xb_ref, done_sem = rest
    elif emit_weights:
        o_ref, wg_out, wu_out, wd_out, xb_ref = rest
    else:
        o_ref, xb_ref = rest
    i = pl.program_id(0)
    j = pl.program_id(1)
    tm = x_ref.shape[0]
    live = True
    if n_done:
        live = i >= n_done

        @pl.when(jnp.logical_and(i < n_done, j == 0))
        def _take_done_tile():
            rows = pl.ds(pl.multiple_of(i * tm, tm), tm)
            copy = pltpu.make_async_copy(done_ref.at[rows, :], o_ref, done_sem)
            copy.start()
            copy.wait()

    @_when_all(live, j == 0)
    def _init():
        def body(rows):
            x = x_ref[rows, :]
            xb_ref[rows, :] = x.astype(BF16)
            o_ref[rows, :] = alpha * x
        _for_row_chunks(tm, body)

    @_when_all(live)
    def _accumulate():
        wg = wg_ref[...].astype(BF16)
        wu = wu_ref[...].astype(BF16)
        wd = wd_ref[...].astype(BF16)
        if emit_weights:
            wg_out[...] = wg
            wu_out[...] = wu
            wd_out[...] = wd
        xb = xb_ref[...]
        gate = _dot(xb, wg)
        up = _dot(xb, wu)
        h = (gate * jax.nn.sigmoid(gate) * (0.5 * up)).astype(BF16)
        o_ref[...] += _dot(h, wd)

    @_when_all(live, j == pl.num_programs(1) - 1)
    def _finish():
        g = g_ref[...]
        b = b_ref[...]

        def body(rows):
            o_ref[rows, :] = _layer_norm_rows(o_ref[rows, :], g, b)
        _for_row_chunks(tm, body)


def _ffn_ln(x, w_gate, w_up, up_col0, w_down, ln_g, ln_b, *, alpha, n_tiles, tf, tm=FFN_TM,
            done=None, emit_weights=False):
    t, d = x.shape
    f = w_down.shape[0]
    assert n_tiles * tm <= t and f % tf == 0 and up_col0 % tf == 0
    n_done = 0 if done is None else done.shape[0] // tm
    assert not (n_done and emit_weights) and (done is None or done.shape == (n_done * tm, d))
    nf = f // tf
    up0 = up_col0 // tf

    def chunk(i, j):
        return jnp.where(i >= n_done, j, 0) if n_done else j

    row_spec = pl.BlockSpec((tm, d), lambda i, j: (i, 0))
    vec = pl.BlockSpec((1, d), lambda i, j: (0, 0))
    gate_spec = pl.BlockSpec((d, tf), lambda i, j: (0, chunk(i, j)))
    down_spec = pl.BlockSpec((tf, d), lambda i, j: (chunk(i, j), 0))
    x_index = lambda i, j: (jnp.maximum(i, n_done), 0)
    in_specs = [
        pl.BlockSpec((tm, d), x_index, pipeline_mode=pl.Buffered(1)) if n_tiles == 1
        else pl.BlockSpec((tm, d), x_index),
        gate_spec,
        pl.BlockSpec((d, tf), lambda i, j: (0, chunk(i, j) + up0)),
        down_spec, vec, vec,
    ]
    args = [x, w_gate, w_up, w_down, ln_g.reshape(1, d), ln_b.reshape(1, d)]
    scratch = [pltpu.VMEM((tm, d), BF16)]
    if n_done:
        in_specs.append(pl.BlockSpec(memory_space=pl.ANY))
        args.append(done)
        scratch.append(pltpu.SemaphoreType.DMA(()))
    out_specs = [row_spec]
    out_shape = [jax.ShapeDtypeStruct((n_tiles * tm, d), F32)]
    if emit_weights:
        out_specs += [gate_spec, gate_spec, down_spec]
        out_shape += [jax.ShapeDtypeStruct((d, f), BF16), jax.ShapeDtypeStruct((d, f), BF16),
                      jax.ShapeDtypeStruct((f, d), BF16)]
    res = pl.pallas_call(
        functools.partial(_ffn_ln_kernel, alpha=alpha, n_done=n_done,
                          emit_weights=emit_weights),
        grid=(n_tiles, nf),
        in_specs=in_specs,
        out_specs=out_specs,
        out_shape=out_shape,
        scratch_shapes=scratch,
        compiler_params=_params("arbitrary", "arbitrary"),
        name="ffn_ln",
    )(*args)
    return res if emit_weights else res[0]


def _rope_tables(seq):
    t = np.arange(seq)
    pos = np.stack([t // GRID_W, t % GRID_W], axis=1).astype(np.float32)
    lane = np.arange(HEAD_DIM)
    half = lane // (HEAD_DIM // 2)
    within = lane % (HEAD_DIM // 2)
    quarter = HEAD_DIM // 4
    first = within < quarter
    freq = within % quarter
    axis_rot = HEAD_DIM // 2
    inv = jnp.asarray(ROPE_THETA, F32) ** (
        -jnp.arange(0, axis_rot, 2, dtype=F32) / axis_rot)
    ang = jnp.asarray(pos)[:, half] * inv[freq][None, :]
    cos = jnp.cos(ang)
    sin = jnp.sin(ang)
    first = jnp.asarray(first)[None, :]
    sin_up = jnp.where(first, -sin, 0.0)
    sin_dn = jnp.where(first, 0.0, sin)
    return cos, sin_up, sin_dn


def _inproj_kernel(x_ref, w_ref, qg_ref, kg_ref, cos_ref, sup_ref, sdn_ref, wfull_ref,
                   o_ref, wgate_ref):
    wgate_ref[...] = wfull_ref[:, W_ATTN_COLS:].astype(BF16)
    xb = x_ref[...].astype(BF16)
    cos = cos_ref[...]
    sup = sup_ref[...]
    sdn = sdn_ref[...]
    scale = HEAD_DIM ** -0.5
    quarter = HEAD_DIM // 4
    pair = 2 * HEAD_DIM

    def norm_rope(z, gain):
        z = z * lax.rsqrt(jnp.mean(z * z, axis=-1, keepdims=True) + RMS_EPS) * gain
        return (z * cos
                + pltpu.roll(z, HEAD_DIM - quarter, 1) * sup
                + pltpu.roll(z, quarter, 1) * sdn)

    for c in range(0, W_ATTN_COLS, pair):
        z2 = _dot(xb, w_ref[:, c:c + pair])
        for hh in range(2):
            col = c + hh * HEAD_DIM
            z = z2[:, hh * HEAD_DIM:(hh + 1) * HEAD_DIM]
            if col < COL_KA:
                z = norm_rope(z, qg_ref[...]) * scale
            elif col < COL_VA:
                z = norm_rope(z, kg_ref[...])
            elif COL_QB <= col < COL_KB or col >= COL_QM:
                z = z * scale
            o_ref[:, col:col + HEAD_DIM] = z.astype(BF16)


def _inproj(x, w_attn, w_full, q_gain, k_gain, seq, *, tm=512):
    t, d = x.shape
    assert t % tm == 0 and seq % tm == 0
    cos, sup, sdn = _rope_tables(seq)
    nseq = seq // tm
    n_steps = t // tm
    slab = d // n_steps
    assert slab * n_steps == d and slab % BF16_ROW_TILE == 0
    n_gate = w_full.shape[1] - W_ATTN_COLS
    tab_spec = pl.BlockSpec((tm, HEAD_DIM), lambda i: (i % nseq, 0))
    gain_spec = pl.BlockSpec((1, HEAD_DIM), lambda i: (0, 0))
    return pl.pallas_call(
        _inproj_kernel,
        grid=(n_steps,),
        in_specs=[
            pl.BlockSpec((tm, d), lambda i: (i, 0)),
            pl.BlockSpec((d, W_ATTN_COLS), lambda i: (0, 0), pipeline_mode=pl.Buffered(1)),
            gain_spec, gain_spec, tab_spec, tab_spec, tab_spec,
            pl.BlockSpec((slab, w_full.shape[1]), lambda i: (i, 0)),
        ],
        out_specs=[pl.BlockSpec((tm, W_ATTN_COLS), lambda i: (i, 0)),
                   pl.BlockSpec((slab, n_gate), lambda i: (i, 0))],
        out_shape=[jax.ShapeDtypeStruct((t, W_ATTN_COLS), BF16),
                   jax.ShapeDtypeStruct((d, n_gate), BF16)],
        compiler_params=_params("arbitrary"),
        name="inproj",
    )(x, w_attn, q_gain.reshape(1, HEAD_DIM), k_gain.reshape(1, HEAD_DIM), cos, sup, sdn,
      w_full)


def _with_ones_column(v):
    lane = lax.broadcasted_iota(jnp.int32, v.shape, 1)
    ones_col = jnp.where(lane == 0, 1.0, 0.0).astype(v.dtype)
    return jnp.concatenate([v, ones_col], axis=1)


def _softmax_pv(s, v_ext):
    m = jnp.max(s, axis=-1, keepdims=True)
    p = jnp.exp(s - m).astype(BF16)
    ov = _dot(p, v_ext)
    return ov[:, :HEAD_DIM] / ov[:, HEAD_DIM:HEAD_DIM + 1]


def _softmax_pv_few_keys(s, v):
    m = jnp.max(s, axis=-1, keepdims=True)
    p = jnp.exp(s - m)
    l = jnp.sum(p, axis=-1, keepdims=True)
    return _dot(p.astype(BF16), v) / l


GQA_GROUP = GQA_Q_HEADS // GQA_KV_HEADS
GQA_SUB_ROWS = 256


def _gqa_kernel(q_ref, k_ref, v_ref, *refs):
    n_cast = (len(refs) - 1) // 2
    o_ref = refs[n_cast]
    v_ext = _with_ones_column(v_ref[...])
    for r in range(0, q_ref.shape[0], GQA_SUB_ROWS):
        rows = slice(r, r + GQA_SUB_ROWS)
        for h in range(GQA_GROUP):
            cols = slice(h * HEAD_DIM, (h + 1) * HEAD_DIM)
            s = _dot_nt(q_ref[rows, cols], k_ref[...])
            o_ref[rows, cols] = _softmax_pv(s, v_ext).astype(BF16)
    for src_ref, dst_ref in zip(refs[:n_cast], refs[n_cast + 1:]):
        dst_ref[...] = src_ref[...].astype(BF16)


def _gqa(z, batch, seq, cast_along=(), *, tq=1024):
    t = z.shape[0]
    nq = seq // tq
    gw = GQA_GROUP * HEAD_DIM
    kcol = COL_KA // HEAD_DIM
    vcol = COL_VA // HEAD_DIM
    n_steps = batch * GQA_KV_HEADS * nq

    def slab(arr):
        rows = arr.shape[0] // n_steps
        assert rows * n_steps == arr.shape[0] and rows % BF16_ROW_TILE == 0
        return pl.BlockSpec((rows, arr.shape[1]),
                            lambda b, g, i: ((b * GQA_KV_HEADS + g) * nq + i, 0))

    slabs = [slab(a) for a in cast_along]
    return pl.pallas_call(
        _gqa_kernel,
        grid=(batch, GQA_KV_HEADS, nq),
        in_specs=[
            pl.BlockSpec((tq, gw), lambda b, g, i: (b * nq + i, g)),
            pl.BlockSpec((seq, HEAD_DIM), lambda b, g, i: (b, kcol + g)),
            pl.BlockSpec((seq, HEAD_DIM), lambda b, g, i: (b, vcol + g)),
        ] + slabs,
        out_specs=[pl.BlockSpec((tq, gw), lambda b, g, i: (b * nq + i, g))] + slabs,
        out_shape=[jax.ShapeDtypeStruct((t, WA_Q), BF16)]
        + [jax.ShapeDtypeStruct(a.shape, BF16) for a in cast_along],
        compiler_params=_params("arbitrary", "arbitrary", "arbitrary"),
        name="gqa",
    )(z, z, z, *cast_along)


NA_Q_ROWS = 4
NA_K_ROWS = NA_Q_ROWS + NA_ROWS_MAX
NA_TQ = NA_Q_ROWS * GRID_W
NA_TK = NA_K_ROWS * GRID_W


def _na_key_row_start(qblock, rows):
    return np.clip(qblock * NA_Q_ROWS - NA_ROWS_MAX // 2, 0, rows - NA_K_ROWS)


def _na_bias_tables(rpb, rows):
    heads, n_dr, n_dc = rpb.shape
    nblocks = rows // NA_Q_ROWS
    kr_win = min(NA_ROWS_MAX, rows)
    period = 2 * GRID_W
    assert period >= GRID_W + NA_COLS and n_dc == 2 * NA_COLS - 1
    v = rpb.astype(F32)
    vpad = jnp.concatenate(
        [v[..., NA_COLS - 1:], jnp.zeros((heads, n_dr, period - n_dc), F32), v[..., :NA_COLS - 1]],
        axis=-1)
    skew = jnp.tile(vpad, (1, 1, GRID_W))[..., :GRID_W * (period - 1)]
    toep = skew.reshape(heads, n_dr, GRID_W, period - 1)[..., :GRID_W]
    qc = np.arange(GRID_W)
    cs = np.clip(qc - NA_COLS // 2, 0, GRID_W - NA_COLS)
    ok_c = (qc[None, :] >= cs[:, None]) & (qc[None, :] < cs[:, None] + NA_COLS)
    toep = jnp.where(jnp.asarray(ok_c)[None, None], toep, MASK_VALUE)
    masked = jnp.full((heads, GRID_W, GRID_W), MASK_VALUE, F32)
    tabs = []
    for qblock in (0, 1, nblocks - 1):
        ks = _na_key_row_start(qblock, rows)
        q_rows = []
        for qi in range(NA_Q_ROWS):
            qr = qblock * NA_Q_ROWS + qi
            rs = np.clip(qr - kr_win // 2, 0, rows - kr_win)
            blocks = []
            for kj in range(NA_K_ROWS):
                kr = ks + kj
                inside = rs <= kr < rs + kr_win
                blocks.append(toep[:, kr - qr + NA_ROWS_MAX - 1] if inside else masked)
            q_rows.append(jnp.concatenate(blocks, axis=-1))
        tabs.append(jnp.concatenate(q_rows, axis=-2))
    return jnp.stack(tabs)


MEM_Q_ROWS = 512


def _local_attn_kernel(qb_ref, kb_ref, vb_ref, bias_ref, qm_ref, mem_ref, wkv_ref,
                       ob_ref, om_ref):
    rows = qb_ref.shape[0] // GRID_W
    n_blocks = rows // NA_Q_ROWS
    for i in range(n_blocks):
        pattern = 0 if i == 0 else (2 if i == n_blocks - 1 else 1)
        qrows = slice(i * NA_TQ, (i + 1) * NA_TQ)
        start = int(_na_key_row_start(i, rows)) * GRID_W
        krows = slice(start, start + NA_TK)
        for h in range(NA_HEADS):
            cols = slice(h * HEAD_DIM, (h + 1) * HEAD_DIM)
            s = _dot_nt(qb_ref[qrows, cols], kb_ref[krows, cols]) + bias_ref[pattern, h]
            v_ext = _with_ones_column(vb_ref[krows, cols])
            ob_ref[qrows, cols] = _softmax_pv(s, v_ext).astype(BF16)
    kvm = _dot(mem_ref[...].astype(BF16), wkv_ref[...].astype(BF16)).astype(BF16)
    for r in range(0, qm_ref.shape[0], MEM_Q_ROWS):
        qrows = slice(r, r + MEM_Q_ROWS)
        for h in range(MEM_HEADS):
            cols = slice(h * HEAD_DIM, (h + 1) * HEAD_DIM)
            s = _dot_nt(qm_ref[qrows, cols], kvm[:, cols])
            v = kvm[:, WM + h * HEAD_DIM:WM + (h + 1) * HEAD_DIM]
            om_ref[qrows, cols] = _softmax_pv_few_keys(s, v).astype(BF16)


def _local_attn(z, rpb, mem2d, w_mem_kv, batch, seq):
    t = z.shape[0]
    mem_len = mem2d.shape[0] // batch
    rows = seq // GRID_W
    assert rows % NA_Q_ROWS == 0 and rows >= NA_K_ROWS and seq % MEM_Q_ROWS == 0
    bias = _na_bias_tables(rpb, rows)

    def per_batch(col_block):
        return pl.BlockSpec((seq, WB), lambda b: (b, col_block))

    def resident(arr):
        return pl.BlockSpec(arr.shape, lambda b: (0,) * arr.ndim, pipeline_mode=pl.Buffered(1))

    return pl.pallas_call(
        _local_attn_kernel,
        grid=(batch,),
        in_specs=[
            per_batch(COL_QB // WB), per_batch(COL_KB // WB), per_batch(COL_VB // WB),
            resident(bias),
            pl.BlockSpec((seq, WM), lambda b: (b, COL_QM // WM)),
            pl.BlockSpec((mem_len, mem2d.shape[1]), lambda b: (b, 0)),
            resident(w_mem_kv),
        ],
        out_specs=[pl.BlockSpec((seq, WB), lambda b: (b, 0)),
                   pl.BlockSpec((seq, WM), lambda b: (b, 0))],
        out_shape=[jax.ShapeDtypeStruct((t, WB), BF16),
                   jax.ShapeDtypeStruct((t, WM), BF16)],
        compiler_params=_params("parallel"),
        name="local_attn",
    )(z, z, z, bias, z, mem2d, w_mem_kv)


MXU_COLS = 256


def _gated_merge_kernel(x_ref, oa_ref, ob_ref, om_ref, wg_ref, bg_ref,
                        woa_ref, wob_ref, wom_ref, y_ref):
    d = x_ref.shape[1]
    xb = x_ref[...].astype(BF16)
    branches = ((oa_ref[...], woa_ref), (ob_ref[...], wob_ref), (om_ref[...], wom_ref))
    for c in range(0, d, MXU_COLS):
        cols = slice(c, c + MXU_COLS)
        y = None
        for k, (o, wo_ref) in enumerate(branches):
            gcols = slice(k * d + c, k * d + c + MXU_COLS)
            gate = jax.nn.sigmoid(_dot(xb, wg_ref[:, gcols]) + bg_ref[:, gcols])
            term = gate * _dot(o, wo_ref[:, cols])
            y = term if y is None else y + term
        y_ref[:, cols] = y.astype(BF16)


def _gated_merge(x, oa, ob, om, w_gate, b_gate, w_oa, w_ob, w_om, *, tm=512):
    t, d = x.shape
    assert t % tm == 0 and d % MXU_COLS == 0 and w_gate.shape == (d, N_BRANCHES * d)

    def rows(width):
        return pl.BlockSpec((tm, width), lambda i: (i, 0))

    def resident(arr):
        return pl.BlockSpec(arr.shape, lambda i: (0, 0), pipeline_mode=pl.Buffered(1))

    b_gate = b_gate.reshape(1, N_BRANCHES * d)
    return pl.pallas_call(
        _gated_merge_kernel,
        grid=(t // tm,),
        in_specs=[rows(d), rows(WA_Q), rows(WB), rows(WM),
                  resident(w_gate), resident(b_gate),
                  resident(w_oa), resident(w_ob), resident(w_om)],
        out_specs=rows(d),
        out_shape=jax.ShapeDtypeStruct((t, d), BF16),
        compiler_params=_params("parallel"),
        name="gated_merge",
    )(x, oa, ob, om, w_gate, b_gate, w_oa, w_ob, w_om)


def _outproj_ln_kernel(x_ref, y_ref, w_ref, g_ref, b_ref, o_ref, *, alpha):
    tm = x_ref.shape[0]
    g = g_ref[...]
    b = b_ref[...]
    for r in range(0, tm, LN_OVERLAP_ROWS):
        rows = slice(r, r + LN_OVERLAP_ROWS)
        y = alpha * x_ref[rows, :] + _dot(y_ref[rows, :], w_ref[...])
        o_ref[rows, :] = _layer_norm_rows(y, g, b)


def _outproj_ln(x, y, w_out, ln_g, ln_b, *, alpha, tm=1024):
    t, d = x.shape
    assert t % tm == 0
    vec = pl.BlockSpec((1, d), lambda i: (0, 0))
    return pl.pallas_call(
        functools.partial(_outproj_ln_kernel, alpha=alpha),
        grid=(t // tm,),
        in_specs=[
            pl.BlockSpec((tm, d), lambda i: (i, 0)),
            pl.BlockSpec((tm, d), lambda i: (i, 0)),
            pl.BlockSpec((d, d), lambda i: (0, 0), pipeline_mode=pl.Buffered(1)),
            vec, vec,
        ],
        out_specs=pl.BlockSpec((tm, d), lambda i: (i, 0)),
        out_shape=jax.ShapeDtypeStruct((t, d), F32),
        compiler_params=_params("parallel"),
        name="outproj_ln",
    )(x, y, w_out, ln_g.reshape(1, d), ln_b.reshape(1, d))


def kernel(x, mem, ln1_g, ln1_b, ffn1_w_gu, ffn1_w_down, w_in, b_gate, q_norm_a, k_norm_a,
           na_rpb, w_mem_kv, w_oa, w_ob, w_om, w_out, ln2_g, ln2_b, ffn2_w_gu, ffn2_w_down,
           ln3_g, ln3_b):
    batch, seq, d = x.shape
    mem_len = mem.shape[1]
    depth = w_in.shape[0]
    alpha = (2 * depth) ** 0.25
    h = x.reshape(batch * seq, d)
    mem2d = mem.reshape(batch * mem_len, d)
    d_ff = ffn1_w_down.shape[1]
    n_tiles = batch * seq // FFN_TM
    for l in range(depth):
        ffn1 = functools.partial(_ffn_ln, ln_g=ln1_g[l], ln_b=ln1_b[l], alpha=alpha)
        head, w_gate1, w_up1, w_down1 = ffn1(
            h, ffn1_w_gu[l], ffn1_w_gu[l], d_ff, ffn1_w_down[l],
            n_tiles=1, tf=FFN_TF_F32, emit_weights=True)
        h = ffn1(h, w_gate1, w_up1, 0, w_down1, n_tiles=n_tiles, tf=FFN_TF, done=head)
        w_attn = w_in[l][:, :W_ATTN_COLS].astype(BF16)
        z, w_gate = _inproj(h, w_attn, w_in[l], q_norm_a[l], k_norm_a[l], seq)
        oa, w_gu2, w_down2, w_oa_b, w_ob_b, w_om_b, w_out_b = _gqa(
            z, batch, seq,
            cast_along=(ffn2_w_gu[l], ffn2_w_down[l], w_oa[l], w_ob[l], w_om[l], w_out[l]))
        ob, om = _local_attn(z, na_rpb[l], mem2d, w_mem_kv[l], batch, seq)
        y = _gated_merge(h, oa, ob, om, w_gate, b_gate[l], w_oa_b, w_ob_b, w_om_b)
        h = _outproj_ln(h, y, w_out_b, ln2_g[l], ln2_b[l], alpha=alpha)
        h = _ffn_ln(h, w_gu2, w_gu2, d_ff, w_down2, ln3_g[l], ln3_b[l], alpha=alpha,
                    n_tiles=n_tiles, tf=FFN_TF)
    return h.reshape(batch, seq, d)
```

```python
import functools

import numpy as np
import jax
import jax.numpy as jnp
from jax import lax
from jax.experimental import pallas as pl
from jax.experimental.pallas import tpu as pltpu

HEAD_DIM = 128
GQA_Q_HEADS = 8
GQA_KV_HEADS = 2
NA_HEADS = 4
MEM_HEADS = 4
GRID_W = 64
NA_ROWS_MAX = 8
NA_COLS = 16
ROPE_THETA = 10000.0
LN_EPS = 1e-5
RMS_EPS = 1e-6
N_BRANCHES = 3
MASK_VALUE = -1e30

WA_Q = GQA_Q_HEADS * HEAD_DIM
WA_KV = GQA_KV_HEADS * HEAD_DIM
WB = NA_HEADS * HEAD_DIM
WM = MEM_HEADS * HEAD_DIM
COL_KA = WA_Q
COL_VA = COL_KA + WA_KV
COL_QB = COL_VA + WA_KV
COL_KB = COL_QB + WB
COL_VB = COL_KB + WB
COL_QM = COL_VB + WB
W_ATTN_COLS = COL_QM + WM

V7X_VMEM_BYTES = 64 * 1024 * 1024
VMEM_LIMIT_BYTES = V7X_VMEM_BYTES - 8 * 1024 * 1024

BF16 = jnp.bfloat16
F32 = jnp.float32
BF16_ROW_TILE = 16


def _params(*semantics):
    return pltpu.CompilerParams(dimension_semantics=semantics,
                                vmem_limit_bytes=VMEM_LIMIT_BYTES)


def _dot(a, b):
    return jnp.dot(a, b, preferred_element_type=F32)


def _dot_nt(a, b):
    return lax.dot_general(a, b, (((1,), (1,)), ((), ())), preferred_element_type=F32)


def _layer_norm_rows(y, g, b):
    mu = jnp.mean(y, axis=-1, keepdims=True)
    yc = y - mu
    var = jnp.mean(yc * yc, axis=-1, keepdims=True)
    return yc * lax.rsqrt(var + LN_EPS) * g + b


LN_ROW_CHUNK = 256


LN_OVERLAP_ROWS = 256


def _for_row_chunks(n_rows, body):
    def step(r, carry):
        body(pl.ds(pl.multiple_of(r * LN_ROW_CHUNK, LN_ROW_CHUNK), LN_ROW_CHUNK))
        return carry
    lax.fori_loop(0, n_rows // LN_ROW_CHUNK, step, 0)


FFN_TM = 1024
FFN_TF = 512
FFN_TF_F32 = 256


def _when_all(*conds):
    traced = [c for c in conds if c is not True]
    if not traced:
        return lambda f: f()
    cond = traced[0]
    for c in traced[1:]:
        cond = jnp.logical_and(cond, c)
    return pl.when(cond)


def _ffn_ln_kernel(x_ref, wg_ref, wu_ref, wd_ref, g_ref, b_ref, *rest, alpha, n_done,
                   emit_weights):
    if n_done:
        done_ref, o_ref, xb_ref, done_sem = rest
    elif emit_weights:
        o_ref, wg_out, wu_out, wd_out, xb_ref = rest
    else:
        o_ref, xb_ref = rest
    i = pl.program_id(0)
    j = pl.program_id(1)
    tm = x_ref.shape[0]
    live = True
    if n_done:
        live = i >= n_done

        @pl.when(jnp.logical_and(i < n_done, j == 0))
        def _take_done_tile():
            rows = pl.ds(pl.multiple_of(i * tm, tm), tm)
            copy = pltpu.make_async_copy(done_ref.at[rows, :], o_ref, done_sem)
            copy.start()
            copy.wait()

    @_when_all(live, j == 0)
    def _init():
        def body(rows):
            x = x_ref[rows, :]
            xb_ref[rows, :] = x.astype(BF16)
            o_ref[rows, :] = alpha * x
        _for_row_chunks(tm, body)

    @_when_all(live)
    def _accumulate():
        wg = wg_ref[...].astype(BF16)
        wu = wu_ref[...].astype(BF16)
        wd = wd_ref[...].astype(BF16)
        if emit_weights:
            wg_out[...] = wg
            wu_out[...] = wu
            wd_out[...] = wd
        xb = xb_ref[...]
        gate = _dot(xb, wg)
        up = _dot(xb, wu)
        h = (gate * jax.nn.sigmoid(gate) * (0.5 * up)).astype(BF16)
        o_ref[...] += _dot(h, wd)

    @_when_all(live, j == pl.num_programs(1) - 1)
    def _finish():
        g = g_ref[...]
        b = b_ref[...]

        def body(rows):
            o_ref[rows, :] = _layer_norm_rows(o_ref[rows, :], g, b)
        _for_row_chunks(tm, body)


def _ffn_ln(x, w_gate, w_up, up_col0, w_down, ln_g, ln_b, *, alpha, n_tiles, tf, tm=FFN_TM,
            done=None, emit_weights=False):
    t, d = x.shape
    f = w_down.shape[0]
    assert n_tiles * tm <= t and f % tf == 0 and up_col0 % tf == 0
    n_done = 0 if done is None else done.shape[0] // tm
    assert not (n_done and emit_weights) and (done is None or done.shape == (n_done * tm, d))
    nf = f // tf
    up0 = up_col0 // tf

    def chunk(i, j):
        return jnp.where(i >= n_done, j, 0) if n_done else j

    row_spec = pl.BlockSpec((tm, d), lambda i, j: (i, 0))
    vec = pl.BlockSpec((1, d), lambda i, j: (0, 0))
    gate_spec = pl.BlockSpec((d, tf), lambda i, j: (0, chunk(i, j)))
    down_spec = pl.BlockSpec((tf, d), lambda i, j: (chunk(i, j), 0))
    x_index = lambda i, j: (jnp.maximum(i, n_done), 0)
    in_specs = [
        pl.BlockSpec((tm, d), x_index),
        gate_spec,
        pl.BlockSpec((d, tf), lambda i, j: (0, chunk(i, j) + up0)),
        down_spec, vec, vec,
    ]
    args = [x, w_gate, w_up, w_down, ln_g.reshape(1, d), ln_b.reshape(1, d)]
    n_piped = len(args)
    scratch = [pltpu.VMEM((tm, d), BF16)]
    if n_done:
        args.append(done)
        scratch.append(pltpu.SemaphoreType.DMA(()))
    out_specs = [row_spec]
    out_shape = [jax.ShapeDtypeStruct((n_tiles * tm, d), F32)]
    if emit_weights:
        out_specs += [gate_spec, gate_spec, down_spec]
        out_shape += [jax.ShapeDtypeStruct((d, f), BF16), jax.ShapeDtypeStruct((d, f), BF16),
                      jax.ShapeDtypeStruct((f, d), BF16)]
    n_out = len(out_shape)
    step = functools.partial(_ffn_ln_kernel, alpha=alpha, n_done=n_done,
                             emit_weights=emit_weights)

    def pipelined(*refs):
        hbm_in = refs[:n_piped]
        hbm_done = refs[n_piped:len(args)]
        hbm_out = refs[len(args):len(args) + n_out]
        scratches = refs[len(args) + n_out:]

        def body(*blocks):
            ins, outs = blocks[:n_piped], blocks[n_piped:n_piped + n_out]
            scr = blocks[n_piped + n_out:]
            if n_done:
                step(*ins, hbm_done[0], outs[0], scr[0], scr[1])
            else:
                step(*ins, *outs, scr[0])

        pltpu.emit_pipeline(body, grid=(n_tiles, nf), in_specs=in_specs,
                            out_specs=out_specs)(*hbm_in, *hbm_out, scratches=scratches)

    hbm = pl.BlockSpec(memory_space=pl.ANY)
    res = pl.pallas_call(
        pipelined,
        in_specs=[hbm] * len(args),
        out_specs=[hbm] * n_out,
        out_shape=out_shape,
        scratch_shapes=scratch,
        compiler_params=pltpu.CompilerParams(vmem_limit_bytes=VMEM_LIMIT_BYTES),
        name="ffn_ln",
    )(*args)
    return res if emit_weights else res[0]


def _rope_tables(seq):
    t = np.arange(seq)
    pos = np.stack([t // GRID_W, t % GRID_W], axis=1).astype(np.float32)
    lane = np.arange(HEAD_DIM)
    half = lane // (HEAD_DIM // 2)
    within = lane % (HEAD_DIM // 2)
    quarter = HEAD_DIM // 4
    first = within < quarter
    freq = within % quarter
    axis_rot = HEAD_DIM // 2
    inv = jnp.asarray(ROPE_THETA, F32) ** (
        -jnp.arange(0, axis_rot, 2, dtype=F32) / axis_rot)
    ang = jnp.asarray(pos)[:, half] * inv[freq][None, :]
    cos = jnp.cos(ang)
    sin = jnp.sin(ang)
    first = jnp.asarray(first)[None, :]
    sin_up = jnp.where(first, -sin, 0.0)
    sin_dn = jnp.where(first, 0.0, sin)
    return cos, sin_up, sin_dn


def _inproj_kernel(x_ref, w_ref, qg_ref, kg_ref, cos_ref, sup_ref, sdn_ref, wfull_ref,
                   o_ref, wgate_ref):
    wgate_ref[...] = wfull_ref[:, W_ATTN_COLS:].astype(BF16)
    xb = x_ref[...].astype(BF16)
    cos = cos_ref[...]
    sup = sup_ref[...]
    sdn = sdn_ref[...]
    scale = HEAD_DIM ** -0.5
    quarter = HEAD_DIM // 4
    pair = 2 * HEAD_DIM

    def norm_rope(z, gain):
        z = z * lax.rsqrt(jnp.mean(z * z, axis=-1, keepdims=True) + RMS_EPS) * gain
        return (z * cos
                + pltpu.roll(z, HEAD_DIM - quarter, 1) * sup
                + pltpu.roll(z, quarter, 1) * sdn)

    for c in range(0, W_ATTN_COLS, pair):
        z2 = _dot(xb, w_ref[:, c:c + pair])
        for hh in range(2):
            col = c + hh * HEAD_DIM
            z = z2[:, hh * HEAD_DIM:(hh + 1) * HEAD_DIM]
            if col < COL_KA:
                z = norm_rope(z, qg_ref[...]) * scale
            elif col < COL_VA:
                z = norm_rope(z, kg_ref[...])
            elif COL_QB <= col < COL_KB or col >= COL_QM:
                z = z * scale
            o_ref[:, col:col + HEAD_DIM] = z.astype(BF16)


def _inproj(x, w_attn, w_full, q_gain, k_gain, seq, *, tm=512):
    t, d = x.shape
    assert t % tm == 0 and seq % tm == 0
    cos, sup, sdn = _rope_tables(seq)
    nseq = seq // tm
    n_steps = t // tm
    slab = d // n_steps
    assert slab * n_steps == d and slab % BF16_ROW_TILE == 0
    n_gate = w_full.shape[1] - W_ATTN_COLS
    tab_spec = pl.BlockSpec((tm, HEAD_DIM), lambda i: (i % nseq, 0))
    gain_spec = pl.BlockSpec((1, HEAD_DIM), lambda i: (0, 0))
    return pl.pallas_call(
        _inproj_kernel,
        grid=(n_steps,),
        in_specs=[
            pl.BlockSpec((tm, d), lambda i: (i, 0)),
            pl.BlockSpec((d, W_ATTN_COLS), lambda i: (0, 0), pipeline_mode=pl.Buffered(1)),
            gain_spec, gain_spec, tab_spec, tab_spec, tab_spec,
            pl.BlockSpec((slab, w_full.shape[1]), lambda i: (i, 0)),
        ],
        out_specs=[pl.BlockSpec((tm, W_ATTN_COLS), lambda i: (i, 0)),
                   pl.BlockSpec((slab, n_gate), lambda i: (i, 0))],
        out_shape=[jax.ShapeDtypeStruct((t, W_ATTN_COLS), BF16),
                   jax.ShapeDtypeStruct((d, n_gate), BF16)],
        compiler_params=_params("arbitrary"),
        name="inproj",
    )(x, w_attn, q_gain.reshape(1, HEAD_DIM), k_gain.reshape(1, HEAD_DIM), cos, sup, sdn,
      w_full)


def _with_ones_column(v):
    lane = lax.broadcasted_iota(jnp.int32, v.shape, 1)
    ones_col = jnp.where(lane == 0, 1.0, 0.0).astype(v.dtype)
    return jnp.concatenate([v, ones_col], axis=1)


def _softmax_pv(s, v_ext):
    m = jnp.max(s, axis=-1, keepdims=True)
    p = jnp.exp(s - m).astype(BF16)
    ov = _dot(p, v_ext)
    return ov[:, :HEAD_DIM] / ov[:, HEAD_DIM:HEAD_DIM + 1]


def _softmax_pv_few_keys(s, v):
    m = jnp.max(s, axis=-1, keepdims=True)
    p = jnp.exp(s - m)
    l = jnp.sum(p, axis=-1, keepdims=True)
    return _dot(p.astype(BF16), v) / l


GQA_GROUP = GQA_Q_HEADS // GQA_KV_HEADS
GQA_SUB_ROWS = 256


def _gqa_kernel(q_ref, k_ref, v_ref, *refs):
    n_cast = (len(refs) - 1) // 2
    o_ref = refs[n_cast]
    v_ext = _with_ones_column(v_ref[...])
    for r in range(0, q_ref.shape[0], GQA_SUB_ROWS):
        rows = slice(r, r + GQA_SUB_ROWS)
        for h in range(GQA_GROUP):
            cols = slice(h * HEAD_DIM, (h + 1) * HEAD_DIM)
            s = _dot_nt(q_ref[rows, cols], k_ref[...])
            o_ref[rows, cols] = _softmax_pv(s, v_ext).astype(BF16)
    for src_ref, dst_ref in zip(refs[:n_cast], refs[n_cast + 1:]):
        dst_ref[...] = src_ref[...].astype(BF16)


def _gqa(z, batch, seq, cast_along=(), *, tq=1024):
    t = z.shape[0]
    nq = seq // tq
    gw = GQA_GROUP * HEAD_DIM
    kcol = COL_KA // HEAD_DIM
    vcol = COL_VA // HEAD_DIM
    n_steps = batch * GQA_KV_HEADS * nq

    def slab(arr):
        rows = arr.shape[0] // n_steps
        assert rows * n_steps == arr.shape[0] and rows % BF16_ROW_TILE == 0
        return pl.BlockSpec((rows, arr.shape[1]),
                            lambda b, g, i: ((b * GQA_KV_HEADS + g) * nq + i, 0))

    slabs = [slab(a) for a in cast_along]
    return pl.pallas_call(
        _gqa_kernel,
        grid=(batch, GQA_KV_HEADS, nq),
        in_specs=[
            pl.BlockSpec((tq, gw), lambda b, g, i: (b * nq + i, g)),
            pl.BlockSpec((seq, HEAD_DIM), lambda b, g, i: (b, kcol + g)),
            pl.BlockSpec((seq, HEAD_DIM), lambda b, g, i: (b, vcol + g)),
        ] + slabs,
        out_specs=[pl.BlockSpec((tq, gw), lambda b, g, i: (b * nq + i, g))] + slabs,
        out_shape=[jax.ShapeDtypeStruct((t, WA_Q), BF16)]
        + [jax.ShapeDtypeStruct(a.shape, BF16) for a in cast_along],
        compiler_params=_params("arbitrary", "arbitrary", "arbitrary"),
        name="gqa",
    )(z, z, z, *cast_along)


NA_Q_ROWS = 4
NA_K_ROWS = NA_Q_ROWS + NA_ROWS_MAX
NA_TQ = NA_Q_ROWS * GRID_W
NA_TK = NA_K_ROWS * GRID_W


def _na_key_row_start(qblock, rows):
    return np.clip(qblock * NA_Q_ROWS - NA_ROWS_MAX // 2, 0, rows - NA_K_ROWS)


def _na_bias_tables(rpb, rows):
    heads, n_dr, n_dc = rpb.shape
    nblocks = rows // NA_Q_ROWS
    kr_win = min(NA_ROWS_MAX, rows)
    period = 2 * GRID_W
    assert period >= GRID_W + NA_COLS and n_dc == 2 * NA_COLS - 1
    v = rpb.astype(F32)
    vpad = jnp.concatenate(
        [v[..., NA_COLS - 1:], jnp.zeros((heads, n_dr, period - n_dc), F32), v[..., :NA_COLS - 1]],
        axis=-1)
    skew = jnp.tile(vpad, (1, 1, GRID_W))[..., :GRID_W * (period - 1)]
    toep = skew.reshape(heads, n_dr, GRID_W, period - 1)[..., :GRID_W]
    qc = np.arange(GRID_W)
    cs = np.clip(qc - NA_COLS // 2, 0, GRID_W - NA_COLS)
    ok_c = (qc[None, :] >= cs[:, None]) & (qc[None, :] < cs[:, None] + NA_COLS)
    toep = jnp.where(jnp.asarray(ok_c)[None, None], toep, MASK_VALUE)
    masked = jnp.full((heads, GRID_W, GRID_W), MASK_VALUE, F32)
    tabs = []
    for qblock in (0, 1, nblocks - 1):
        ks = _na_key_row_start(qblock, rows)
        q_rows = []
        for qi in range(NA_Q_ROWS):
            qr = qblock * NA_Q_ROWS + qi
            rs = np.clip(qr - kr_win // 2, 0, rows - kr_win)
            blocks = []
            for kj in range(NA_K_ROWS):
                kr = ks + kj
                inside = rs <= kr < rs + kr_win
                blocks.append(toep[:, kr - qr + NA_ROWS_MAX - 1] if inside else masked)
            q_rows.append(jnp.concatenate(blocks, axis=-1))
        tabs.append(jnp.concatenate(q_rows, axis=-2))
    return jnp.stack(tabs)


MEM_Q_ROWS = 512


def _local_attn_kernel(qb_ref, kb_ref, vb_ref, bias_ref, qm_ref, mem_ref, wkv_ref,
                       ob_ref, om_ref):
    rows = qb_ref.shape[0] // GRID_W
    n_blocks = rows // NA_Q_ROWS
    for i in range(n_blocks):
        pattern = 0 if i == 0 else (2 if i == n_blocks - 1 else 1)
        qrows = slice(i * NA_TQ, (i + 1) * NA_TQ)
        start = int(_na_key_row_start(i, rows)) * GRID_W
        krows = slice(start, start + NA_TK)
        for h in range(NA_HEADS):
            cols = slice(h * HEAD_DIM, (h + 1) * HEAD_DIM)
            s = _dot_nt(qb_ref[qrows, cols], kb_ref[krows, cols]) + bias_ref[pattern, h]
            v_ext = _with_ones_column(vb_ref[krows, cols])
            ob_ref[qrows, cols] = _softmax_pv(s, v_ext).astype(BF16)
    kvm = _dot(mem_ref[...].astype(BF16), wkv_ref[...].astype(BF16)).astype(BF16)
    for r in range(0, qm_ref.shape[0], MEM_Q_ROWS):
        qrows = slice(r, r + MEM_Q_ROWS)
        for h in range(MEM_HEADS):
            cols = slice(h * HEAD_DIM, (h + 1) * HEAD_DIM)
            s = _dot_nt(qm_ref[qrows, cols], kvm[:, cols])
            v = kvm[:, WM + h * HEAD_DIM:WM + (h + 1) * HEAD_DIM]
            om_ref[qrows, cols] = _softmax_pv_few_keys(s, v).astype(BF16)


def _local_attn(z, rpb, mem2d, w_mem_kv, batch, seq):
    t = z.shape[0]
    mem_len = mem2d.shape[0] // batch
    rows = seq // GRID_W
    assert rows % NA_Q_ROWS == 0 and rows >= NA_K_ROWS and seq % MEM_Q_ROWS == 0
    bias = _na_bias_tables(rpb, rows)

    def per_batch(col_block):
        return pl.BlockSpec((seq, WB), lambda b: (b, col_block))

    def resident(arr):
        return pl.BlockSpec(arr.shape, lambda b: (0,) * arr.ndim, pipeline_mode=pl.Buffered(1))

    return pl.pallas_call(
        _local_attn_kernel,
        grid=(batch,),
        in_specs=[
            per_batch(COL_QB // WB), per_batch(COL_KB // WB), per_batch(COL_VB // WB),
            resident(bias),
            pl.BlockSpec((seq, WM), lambda b: (b, COL_QM // WM)),
            pl.BlockSpec((mem_len, mem2d.shape[1]), lambda b: (b, 0)),
            resident(w_mem_kv),
        ],
        out_specs=[pl.BlockSpec((seq, WB), lambda b: (b, 0)),
                   pl.BlockSpec((seq, WM), lambda b: (b, 0))],
        out_shape=[jax.ShapeDtypeStruct((t, WB), BF16),
                   jax.ShapeDtypeStruct((t, WM), BF16)],
        compiler_params=_params("parallel"),
        name="local_attn",
    )(z, z, z, bias, z, mem2d, w_mem_kv)


MXU_COLS = 256


def _gated_merge_kernel(x_ref, oa_ref, ob_ref, om_ref, wg_ref, bg_ref,
                        woa_ref, wob_ref, wom_ref, y_ref):
    d = x_ref.shape[1]
    xb = x_ref[...].astype(BF16)
    branches = ((oa_ref[...], woa_ref), (ob_ref[...], wob_ref), (om_ref[...], wom_ref))
    for c in range(0, d, MXU_COLS):
        cols = slice(c, c + MXU_COLS)
        y = None
        for k, (o, wo_ref) in enumerate(branches):
            gcols = slice(k * d + c, k * d + c + MXU_COLS)
            gate = jax.nn.sigmoid(_dot(xb, wg_ref[:, gcols]) + bg_ref[:, gcols])
            term = gate * _dot(o, wo_ref[:, cols])
            y = term if y is None else y + term
        y_ref[:, cols] = y.astype(BF16)


def _gated_merge(x, oa, ob, om, w_gate, b_gate, w_oa, w_ob, w_om, *, tm=512):
    t, d = x.shape
    assert t % tm == 0 and d % MXU_COLS == 0 and w_gate.shape == (d, N_BRANCHES * d)

    def rows(width):
        return pl.BlockSpec((tm, width), lambda i: (i, 0))

    def resident(arr):
        return pl.BlockSpec(arr.shape, lambda i: (0, 0), pipeline_mode=pl.Buffered(1))

    b_gate = b_gate.reshape(1, N_BRANCHES * d)
    return pl.pallas_call(
        _gated_merge_kernel,
        grid=(t // tm,),
        in_specs=[rows(d), rows(WA_Q), rows(WB), rows(WM),
                  resident(w_gate), resident(b_gate),
                  resident(w_oa), resident(w_ob), resident(w_om)],
        out_specs=rows(d),
        out_shape=jax.ShapeDtypeStruct((t, d), BF16),
        compiler_params=_params("parallel"),
        name="gated_merge",
    )(x, oa, ob, om, w_gate, b_gate, w_oa, w_ob, w_om)


def _outproj_ln_kernel(x_ref, y_ref, w_ref, g_ref, b_ref, o_ref, *, alpha):
    tm = x_ref.shape[0]
    g = g_ref[...]
    b = b_ref[...]
    for r in range(0, tm, LN_OVERLAP_ROWS):
        rows = slice(r, r + LN_OVERLAP_ROWS)
        y = alpha * x_ref[rows, :] + _dot(y_ref[rows, :], w_ref[...])
        o_ref[rows, :] = _layer_norm_rows(y, g, b)


def _outproj_ln(x, y, w_out, ln_g, ln_b, *, alpha, tm=1024):
    t, d = x.shape
    assert t % tm == 0
    vec = pl.BlockSpec((1, d), lambda i: (0, 0))
    return pl.pallas_call(
        functools.partial(_outproj_ln_kernel, alpha=alpha),
        grid=(t // tm,),
        in_specs=[
            pl.BlockSpec((tm, d), lambda i: (i, 0)),
            pl.BlockSpec((tm, d), lambda i: (i, 0)),
            pl.BlockSpec((d, d), lambda i: (0, 0), pipeline_mode=pl.Buffered(1)),
            vec, vec,
        ],
        out_specs=pl.BlockSpec((tm, d), lambda i: (i, 0)),
        out_shape=jax.ShapeDtypeStruct((t, d), F32),
        compiler_params=_params("parallel"),
        name="outproj_ln",
    )(x, y, w_out, ln_g.reshape(1, d), ln_b.reshape(1, d))


def kernel(x, mem, ln1_g, ln1_b, ffn1_w_gu, ffn1_w_down, w_in, b_gate, q_norm_a, k_norm_a,
           na_rpb, w_mem_kv, w_oa, w_ob, w_om, w_out, ln2_g, ln2_b, ffn2_w_gu, ffn2_w_down,
           ln3_g, ln3_b):
    batch, seq, d = x.shape
    mem_len = mem.shape[1]
    depth = w_in.shape[0]
    alpha = (2 * depth) ** 0.25
    h = x.reshape(batch * seq, d)
    mem2d = mem.reshape(batch * mem_len, d)
    d_ff = ffn1_w_down.shape[1]
    n_tiles = batch * seq // FFN_TM
    for l in range(depth):
        ffn1 = functools.partial(_ffn_ln, ln_g=ln1_g[l], ln_b=ln1_b[l], alpha=alpha)
        head, w_gate1, w_up1, w_down1 = ffn1(
            h, ffn1_w_gu[l], ffn1_w_gu[l], d_ff, ffn1_w_down[l],
            n_tiles=1, tf=FFN_TF_F32, emit_weights=True)
        h = ffn1(h, w_gate1, w_up1, 0, w_down1, n_tiles=n_tiles, tf=FFN_TF, done=head)
        w_attn = w_in[l][:, :W_ATTN_COLS].astype(BF16)
        z, w_gate = _inproj(h, w_attn, w_in[l], q_norm_a[l], k_norm_a[l], seq)
        oa, w_gu2, w_down2, w_oa_b, w_ob_b, w_om_b, w_out_b = _gqa(
            z, batch, seq,
            cast_along=(ffn2_w_gu[l], ffn2_w_down[l], w_oa[l], w_ob[l], w_om[l], w_out[l]))
        ob, om = _local_attn(z, na_rpb[l], mem2d, w_mem_kv[l], batch, seq)
        y = _gated_merge(h, oa, ob, om, w_gate, b_gate[l], w_oa_b, w_ob_b, w_om_b)
        h = _outproj_ln(h, y, w_out_b, ln2_g[l], ln2_b[l], alpha=alpha)
        h = _ffn_ln(h, w_gu2, w_gu2, d_ff, w_down2, ln3_g[l], ln3_b[l], alpha=alpha,
                    n_tiles=n_tiles, tf=FFN_TF)
    return h.reshape(batch, seq, d)
```

```python
import functools

import numpy as np
import jax
import jax.numpy as jnp
from jax import lax
from jax.experimental import pallas as pl
from jax.experimental.pallas import tpu as pltpu

HEAD_DIM = 128
GQA_Q_HEADS = 8
GQA_KV_HEADS = 2
NA_HEADS = 4
MEM_HEADS = 4
GRID_W = 64
NA_ROWS_MAX = 8
NA_COLS = 16
ROPE_THETA = 10000.0
LN_EPS = 1e-5
RMS_EPS = 1e-6
N_BRANCHES = 3
MASK_VALUE = -1e30

WA_Q = GQA_Q_HEADS * HEAD_DIM
WA_KV = GQA_KV_HEADS * HEAD_DIM
WB = NA_HEADS * HEAD_DIM
WM = MEM_HEADS * HEAD_DIM
COL_KA = WA_Q
COL_VA = COL_KA + WA_KV
COL_QB = COL_VA + WA_KV
COL_KB = COL_QB + WB
COL_VB = COL_KB + WB
COL_QM = COL_VB + WB
W_ATTN_COLS = COL_QM + WM

V7X_VMEM_BYTES = 64 * 1024 * 1024
VMEM_LIMIT_BYTES = V7X_VMEM_BYTES - 8 * 1024 * 1024

BF16 = jnp.bfloat16
F32 = jnp.float32
BF16_ROW_TILE = 16


def _params(*semantics):
    return pltpu.CompilerParams(dimension_semantics=semantics,
                                vmem_limit_bytes=VMEM_LIMIT_BYTES)


def _dot(a, b):
    return jnp.dot(a, b, preferred_element_type=F32)


def _dot_nt(a, b):
    return lax.dot_general(a, b, (((1,), (1,)), ((), ())), preferred_element_type=F32)


def _layer_norm_rows(y, g, b):
    mu = jnp.mean(y, axis=-1, keepdims=True)
    yc = y - mu
    var = jnp.mean(yc * yc, axis=-1, keepdims=True)
    return yc * lax.rsqrt(var + LN_EPS) * g + b


LN_ROW_CHUNK = 256


LN_OVERLAP_ROWS = 256


def _for_row_chunks(n_rows, body):
    def step(r, carry):
        body(pl.ds(pl.multiple_of(r * LN_ROW_CHUNK, LN_ROW_CHUNK), LN_ROW_CHUNK))
        return carry
    lax.fori_loop(0, n_rows // LN_ROW_CHUNK, step, 0)


FFN_TM = 1024
FFN_TF = 512
FFN_TF_F32 = 256


def _when_all(*conds):
    traced = [c for c in conds if c is not True]
    if not traced:
        return lambda f: f()
    cond = traced[0]
    for c in traced[1:]:
        cond = jnp.logical_and(cond, c)
    return pl.when(cond)


def _ffn_ln_kernel(x_ref, wg_ref, wu_ref, wd_ref, g_ref, b_ref, *rest, alpha, n_done,
                   emit_weights):
    if n_done:
        done_ref, o_ref, xb_ref, done_sem = rest
    elif emit_weights:
        o_ref, wg_out, wu_out, wd_out, xb_ref = rest
    else:
        o_ref, xb_ref = rest
    i = pl.program_id(0)
    j = pl.program_id(1)
    tm = x_ref.shape[0]
    live = True
    if n_done:
        live = i >= n_done

        @pl.when(jnp.logical_and(i < n_done, j == 0))
        def _take_done_tile():
            rows = pl.ds(pl.multiple_of(i * tm, tm), tm)
            copy = pltpu.make_async_copy(done_ref.at[rows, :], o_ref, done_sem)
            copy.start()
            copy.wait()

    @_when_all(live, j == 0)
    def _init():
        def body(rows):
            x = x_ref[rows, :]
            xb_ref[rows, :] = x.astype(BF16)
            o_ref[rows, :] = alpha * x
        _for_row_chunks(tm, body)

    @_when_all(live)
    def _accumulate():
        wg = wg_ref[...].astype(BF16)
        wu = wu_ref[...].astype(BF16)
        wd = wd_ref[...].astype(BF16)
        if emit_weights:
            wg_out[...] = wg
            wu_out[...] = wu
            wd_out[...] = wd
        xb = xb_ref[...]
        gate = _dot(xb, wg)
        up = _dot(xb, wu)
        h = (gate * jax.nn.sigmoid(gate) * (0.5 * up)).astype(BF16)
        o_ref[...] += _dot(h, wd)

    @_when_all(live, j == pl.num_programs(1) - 1)
    def _finish():
        g = g_ref[...]
        b = b_ref[...]

        def body(rows):
            o_ref[rows, :] = _layer_norm_rows(o_ref[rows, :], g, b)
        _for_row_chunks(tm, body)


def _ffn_ln(x, w_gate, w_up, up_col0, w_down, ln_g, ln_b, *, alpha, n_tiles, tf, tm=FFN_TM,
            done=None, emit_weights=False):
    t, d = x.shape
    f = w_down.shape[0]
    assert n_tiles * tm <= t and f % tf == 0 and up_col0 % tf == 0
    n_done = 0 if done is None else done.shape[0] // tm
    assert not (n_done and emit_weights) and (done is None or done.shape == (n_done * tm, d))
    nf = f // tf
    up0 = up_col0 // tf

    def chunk(i, j):
        return jnp.where(i >= n_done, j, 0) if n_done else j

    row_spec = pl.BlockSpec((tm, d), lambda i, j: (i, 0))
    vec = pl.BlockSpec((1, d), lambda i, j: (0, 0))
    gate_spec = pl.BlockSpec((d, tf), lambda i, j: (0, chunk(i, j)))
    down_spec = pl.BlockSpec((tf, d), lambda i, j: (chunk(i, j), 0))
    x_index = lambda i, j: (jnp.maximum(i, n_done), 0)
    in_specs = [
        pl.BlockSpec((tm, d), x_index, pipeline_mode=pl.Buffered(2, use_lookahead=True))
        if n_tiles > 1 else pl.BlockSpec((tm, d), x_index),
        gate_spec,
        pl.BlockSpec((d, tf), lambda i, j: (0, chunk(i, j) + up0)),
        down_spec, vec, vec,
    ]
    args = [x, w_gate, w_up, w_down, ln_g.reshape(1, d), ln_b.reshape(1, d)]
    n_piped = len(args)
    scratch = [pltpu.VMEM((tm, d), BF16)]
    if n_done:
        args.append(done)
        scratch.append(pltpu.SemaphoreType.DMA(()))
    out_specs = [row_spec]
    out_shape = [jax.ShapeDtypeStruct((n_tiles * tm, d), F32)]
    if emit_weights:
        out_specs += [gate_spec, gate_spec, down_spec]
        out_shape += [jax.ShapeDtypeStruct((d, f), BF16), jax.ShapeDtypeStruct((d, f), BF16),
                      jax.ShapeDtypeStruct((f, d), BF16)]
    n_out = len(out_shape)
    step = functools.partial(_ffn_ln_kernel, alpha=alpha, n_done=n_done,
                             emit_weights=emit_weights)

    def pipelined(*refs):
        hbm_in = refs[:n_piped]
        hbm_done = refs[n_piped:len(args)]
        hbm_out = refs[len(args):len(args) + n_out]
        scratches = refs[len(args) + n_out:]

        def body(*blocks):
            ins, outs = blocks[:n_piped], blocks[n_piped:n_piped + n_out]
            scr = blocks[n_piped + n_out:]
            if n_done:
                step(*ins, hbm_done[0], outs[0], scr[0], scr[1])
            else:
                step(*ins, *outs, scr[0])

        pltpu.emit_pipeline(body, grid=(n_tiles, nf), in_specs=in_specs,
                            out_specs=out_specs)(*hbm_in, *hbm_out, scratches=scratches)

    hbm = pl.BlockSpec(memory_space=pl.ANY)
    res = pl.pallas_call(
        pipelined,
        in_specs=[hbm] * len(args),
        out_specs=[hbm] * n_out,
        out_shape=out_shape,
        scratch_shapes=scratch,
        compiler_params=pltpu.CompilerParams(vmem_limit_bytes=VMEM_LIMIT_BYTES),
        name="ffn_ln",
    )(*args)
    return res if emit_weights else res[0]


def _rope_tables(seq):
    t = np.arange(seq)
    pos = np.stack([t // GRID_W, t % GRID_W], axis=1).astype(np.float32)
    lane = np.arange(HEAD_DIM)
    half = lane // (HEAD_DIM // 2)
    within = lane % (HEAD_DIM // 2)
    quarter = HEAD_DIM // 4
    first = within < quarter
    freq = within % quarter
    axis_rot = HEAD_DIM // 2
    inv = jnp.asarray(ROPE_THETA, F32) ** (
        -jnp.arange(0, axis_rot, 2, dtype=F32) / axis_rot)
    ang = jnp.asarray(pos)[:, half] * inv[freq][None, :]
    cos = jnp.cos(ang)
    sin = jnp.sin(ang)
    first = jnp.asarray(first)[None, :]
    sin_up = jnp.where(first, -sin, 0.0)
    sin_dn = jnp.where(first, 0.0, sin)
    return cos, sin_up, sin_dn


def _inproj_kernel(x_ref, w_ref, qg_ref, kg_ref, cos_ref, sup_ref, sdn_ref, wfull_ref,
                   o_ref, wgate_ref):
    wgate_ref[...] = wfull_ref[:, W_ATTN_COLS:].astype(BF16)
    xb = x_ref[...].astype(BF16)
    cos = cos_ref[...]
    sup = sup_ref[...]
    sdn = sdn_ref[...]
    scale = HEAD_DIM ** -0.5
    quarter = HEAD_DIM // 4
    pair = 2 * HEAD_DIM

    def norm_rope(z, gain):
        z = z * lax.rsqrt(jnp.mean(z * z, axis=-1, keepdims=True) + RMS_EPS) * gain
        return (z * cos
                + pltpu.roll(z, HEAD_DIM - quarter, 1) * sup
                + pltpu.roll(z, quarter, 1) * sdn)

    for c in range(0, W_ATTN_COLS, pair):
        z2 = _dot(xb, w_ref[:, c:c + pair])
        for hh in range(2):
            col = c + hh * HEAD_DIM
            z = z2[:, hh * HEAD_DIM:(hh + 1) * HEAD_DIM]
            if col < COL_KA:
                z = norm_rope(z, qg_ref[...]) * scale
            elif col < COL_VA:
                z = norm_rope(z, kg_ref[...])
            elif COL_QB <= col < COL_KB or col >= COL_QM:
                z = z * scale
            o_ref[:, col:col + HEAD_DIM] = z.astype(BF16)


def _inproj(x, w_attn, w_full, q_gain, k_gain, seq, *, tm=512):
    t, d = x.shape
    assert t % tm == 0 and seq % tm == 0
    cos, sup, sdn = _rope_tables(seq)
    nseq = seq // tm
    n_steps = t // tm
    slab = d // n_steps
    assert slab * n_steps == d and slab % BF16_ROW_TILE == 0
    n_gate = w_full.shape[1] - W_ATTN_COLS
    tab_spec = pl.BlockSpec((tm, HEAD_DIM), lambda i: (i % nseq, 0))
    gain_spec = pl.BlockSpec((1, HEAD_DIM), lambda i: (0, 0))
    return pl.pallas_call(
        _inproj_kernel,
        grid=(n_steps,),
        in_specs=[
            pl.BlockSpec((tm, d), lambda i: (i, 0)),
            pl.BlockSpec((d, W_ATTN_COLS), lambda i: (0, 0), pipeline_mode=pl.Buffered(1)),
            gain_spec, gain_spec, tab_spec, tab_spec, tab_spec,
            pl.BlockSpec((slab, w_full.shape[1]), lambda i: (i, 0)),
        ],
        out_specs=[pl.BlockSpec((tm, W_ATTN_COLS), lambda i: (i, 0)),
                   pl.BlockSpec((slab, n_gate), lambda i: (i, 0))],
        out_shape=[jax.ShapeDtypeStruct((t, W_ATTN_COLS), BF16),
                   jax.ShapeDtypeStruct((d, n_gate), BF16)],
        compiler_params=_params("arbitrary"),
        name="inproj",
    )(x, w_attn, q_gain.reshape(1, HEAD_DIM), k_gain.reshape(1, HEAD_DIM), cos, sup, sdn,
      w_full)


def _with_ones_column(v):
    lane = lax.broadcasted_iota(jnp.int32, v.shape, 1)
    ones_col = jnp.where(lane == 0, 1.0, 0.0).astype(v.dtype)
    return jnp.concatenate([v, ones_col], axis=1)


def _softmax_pv(s, v_ext):
    m = jnp.max(s, axis=-1, keepdims=True)
    p = jnp.exp(s - m).astype(BF16)
    ov = _dot(p, v_ext)
    return ov[:, :HEAD_DIM] / ov[:, HEAD_DIM:HEAD_DIM + 1]


def _softmax_pv_few_keys(s, v):
    m = jnp.max(s, axis=-1, keepdims=True)
    p = jnp.exp(s - m)
    l = jnp.sum(p, axis=-1, keepdims=True)
    return _dot(p.astype(BF16), v) / l


GQA_GROUP = GQA_Q_HEADS // GQA_KV_HEADS
GQA_SUB_ROWS = 256


def _gqa_kernel(q_ref, k_ref, v_ref, *refs):
    n_cast = (len(refs) - 1) // 2
    o_ref = refs[n_cast]
    v_ext = _with_ones_column(v_ref[...])
    for r in range(0, q_ref.shape[0], GQA_SUB_ROWS):
        rows = slice(r, r + GQA_SUB_ROWS)
        for h in range(GQA_GROUP):
            cols = slice(h * HEAD_DIM, (h + 1) * HEAD_DIM)
            s = _dot_nt(q_ref[rows, cols], k_ref[...])
            o_ref[rows, cols] = _softmax_pv(s, v_ext).astype(BF16)
    for src_ref, dst_ref in zip(refs[:n_cast], refs[n_cast + 1:]):
        dst_ref[...] = src_ref[...].astype(BF16)


def _gqa(z, batch, seq, cast_along=(), *, tq=1024):
    t = z.shape[0]
    nq = seq // tq
    gw = GQA_GROUP * HEAD_DIM
    kcol = COL_KA // HEAD_DIM
    vcol = COL_VA // HEAD_DIM
    n_steps = batch * GQA_KV_HEADS * nq

    def slab(arr):
        rows = arr.shape[0] // n_steps
        assert rows * n_steps == arr.shape[0] and rows % BF16_ROW_TILE == 0
        return pl.BlockSpec((rows, arr.shape[1]),
                            lambda b, g, i: ((b * GQA_KV_HEADS + g) * nq + i, 0))

    slabs = [slab(a) for a in cast_along]
    return pl.pallas_call(
        _gqa_kernel,
        grid=(batch, GQA_KV_HEADS, nq),
        in_specs=[
            pl.BlockSpec((tq, gw), lambda b, g, i: (b * nq + i, g)),
            pl.BlockSpec((seq, HEAD_DIM), lambda b, g, i: (b, kcol + g)),
            pl.BlockSpec((seq, HEAD_DIM), lambda b, g, i: (b, vcol + g)),
        ] + slabs,
        out_specs=[pl.BlockSpec((tq, gw), lambda b, g, i: (b * nq + i, g))] + slabs,
        out_shape=[jax.ShapeDtypeStruct((t, WA_Q), BF16)]
        + [jax.ShapeDtypeStruct(a.shape, BF16) for a in cast_along],
        compiler_params=_params("arbitrary", "arbitrary", "arbitrary"),
        name="gqa",
    )(z, z, z, *cast_along)


NA_Q_ROWS = 4
NA_K_ROWS = NA_Q_ROWS + NA_ROWS_MAX
NA_TQ = NA_Q_ROWS * GRID_W
NA_TK = NA_K_ROWS * GRID_W


def _na_key_row_start(qblock, rows):
    return np.clip(qblock * NA_Q_ROWS - NA_ROWS_MAX // 2, 0, rows - NA_K_ROWS)


def _na_bias_tables(rpb, rows):
    heads, n_dr, n_dc = rpb.shape
    nblocks = rows // NA_Q_ROWS
    kr_win = min(NA_ROWS_MAX, rows)
    period = 2 * GRID_W
    assert period >= GRID_W + NA_COLS and n_dc == 2 * NA_COLS - 1
    v = rpb.astype(F32)
    vpad = jnp.concatenate(
        [v[..., NA_COLS - 1:], jnp.zeros((heads, n_dr, period - n_dc), F32), v[..., :NA_COLS - 1]],
        axis=-1)
    skew = jnp.tile(vpad, (1, 1, GRID_W))[..., :GRID_W * (period - 1)]
    toep = skew.reshape(heads, n_dr, GRID_W, period - 1)[..., :GRID_W]
    qc = np.arange(GRID_W)
    cs = np.clip(qc - NA_COLS // 2, 0, GRID_W - NA_COLS)
    ok_c = (qc[None, :] >= cs[:, None]) & (qc[None, :] < cs[:, None] + NA_COLS)
    toep = jnp.where(jnp.asarray(ok_c)[None, None], toep, MASK_VALUE)
    masked = jnp.full((heads, GRID_W, GRID_W), MASK_VALUE, F32)
    tabs = []
    for qblock in (0, 1, nblocks - 1):
        ks = _na_key_row_start(qblock, rows)
        q_rows = []
        for qi in range(NA_Q_ROWS):
            qr = qblock * NA_Q_ROWS + qi
            rs = np.clip(qr - kr_win // 2, 0, rows - kr_win)
            blocks = []
            for kj in range(NA_K_ROWS):
                kr = ks + kj
                inside = rs <= kr < rs + kr_win
                blocks.append(toep[:, kr - qr + NA_ROWS_MAX - 1] if inside else masked)
            q_rows.append(jnp.concatenate(blocks, axis=-1))
        tabs.append(jnp.concatenate(q_rows, axis=-2))
    return jnp.stack(tabs)


MEM_Q_ROWS = 512


def _local_attn_kernel(qb_ref, kb_ref, vb_ref, bias_ref, qm_ref, mem_ref, wkv_ref,
                       ob_ref, om_ref):
    rows = qb_ref.shape[0] // GRID_W
    n_blocks = rows // NA_Q_ROWS
    for i in range(n_blocks):
        pattern = 0 if i == 0 else (2 if i == n_blocks - 1 else 1)
        qrows = slice(i * NA_TQ, (i + 1) * NA_TQ)
        start = int(_na_key_row_start(i, rows)) * GRID_W
        krows = slice(start, start + NA_TK)
        for h in range(NA_HEADS):
            cols = slice(h * HEAD_DIM, (h + 1) * HEAD_DIM)
            s = _dot_nt(qb_ref[qrows, cols], kb_ref[krows, cols]) + bias_ref[pattern, h]
            v_ext = _with_ones_column(vb_ref[krows, cols])
            ob_ref[qrows, cols] = _softmax_pv(s, v_ext).astype(BF16)
    kvm = _dot(mem_ref[...].astype(BF16), wkv_ref[...].astype(BF16)).astype(BF16)
    for r in range(0, qm_ref.shape[0], MEM_Q_ROWS):
        qrows = slice(r, r + MEM_Q_ROWS)
        for h in range(MEM_HEADS):
            cols = slice(h * HEAD_DIM, (h + 1) * HEAD_DIM)
            s = _dot_nt(qm_ref[qrows, cols], kvm[:, cols])
            v = kvm[:, WM + h * HEAD_DIM:WM + (h + 1) * HEAD_DIM]
            om_ref[qrows, cols] = _softmax_pv_few_keys(s, v).astype(BF16)


def _local_attn(z, rpb, mem2d, w_mem_kv, batch, seq):
    t = z.shape[0]
    mem_len = mem2d.shape[0] // batch
    rows = seq // GRID_W
    assert rows % NA_Q_ROWS == 0 and rows >= NA_K_ROWS and seq % MEM_Q_ROWS == 0
    bias = _na_bias_tables(rpb, rows)

    def per_batch(col_block):
        return pl.BlockSpec((seq, WB), lambda b: (b, col_block))

    def resident(arr):
        return pl.BlockSpec(arr.shape, lambda b: (0,) * arr.ndim, pipeline_mode=pl.Buffered(1))

    return pl.pallas_call(
        _local_attn_kernel,
        grid=(batch,),
        in_specs=[
            per_batch(COL_QB // WB), per_batch(COL_KB // WB), per_batch(COL_VB // WB),
            resident(bias),
            pl.BlockSpec((seq, WM), lambda b: (b, COL_QM // WM)),
            pl.BlockSpec((mem_len, mem2d.shape[1]), lambda b: (b, 0)),
            resident(w_mem_kv),
        ],
        out_specs=[pl.BlockSpec((seq, WB), lambda b: (b, 0)),
                   pl.BlockSpec((seq, WM), lambda b: (b, 0))],
        out_shape=[jax.ShapeDtypeStruct((t, WB), BF16),
                   jax.ShapeDtypeStruct((t, WM), BF16)],
        compiler_params=_params("parallel"),
        name="local_attn",
    )(z, z, z, bias, z, mem2d, w_mem_kv)


MXU_COLS = 256


def _gated_merge_kernel(x_ref, oa_ref, ob_ref, om_ref, wg_ref, bg_ref,
                        woa_ref, wob_ref, wom_ref, y_ref):
    d = x_ref.shape[1]
    xb = x_ref[...].astype(BF16)
    branches = ((oa_ref[...], woa_ref), (ob_ref[...], wob_ref), (om_ref[...], wom_ref))
    for c in range(0, d, MXU_COLS):
        cols = slice(c, c + MXU_COLS)
        y = None
        for k, (o, wo_ref) in enumerate(branches):
            gcols = slice(k * d + c, k * d + c + MXU_COLS)
            gate = jax.nn.sigmoid(_dot(xb, wg_ref[:, gcols]) + bg_ref[:, gcols])
            term = gate * _dot(o, wo_ref[:, cols])
            y = term if y is None else y + term
        y_ref[:, cols] = y.astype(BF16)


def _gated_merge(x, oa, ob, om, w_gate, b_gate, w_oa, w_ob, w_om, *, tm=512):
    t, d = x.shape
    assert t % tm == 0 and d % MXU_COLS == 0 and w_gate.shape == (d, N_BRANCHES * d)

    def rows(width):
        return pl.BlockSpec((tm, width), lambda i: (i, 0))

    def resident(arr):
        return pl.BlockSpec(arr.shape, lambda i: (0, 0), pipeline_mode=pl.Buffered(1))

    b_gate = b_gate.reshape(1, N_BRANCHES * d)
    return pl.pallas_call(
        _gated_merge_kernel,
        grid=(t // tm,),
        in_specs=[rows(d), rows(WA_Q), rows(WB), rows(WM),
                  resident(w_gate), resident(b_gate),
                  resident(w_oa), resident(w_ob), resident(w_om)],
        out_specs=rows(d),
        out_shape=jax.ShapeDtypeStruct((t, d), BF16),
        compiler_params=_params("parallel"),
        name="gated_merge",
    )(x, oa, ob, om, w_gate, b_gate, w_oa, w_ob, w_om)


def _outproj_ln_kernel(x_ref, y_ref, w_ref, g_ref, b_ref, o_ref, *, alpha):
    tm = x_ref.shape[0]
    g = g_ref[...]
    b = b_ref[...]
    for r in range(0, tm, LN_OVERLAP_ROWS):
        rows = slice(r, r + LN_OVERLAP_ROWS)
        y = alpha * x_ref[rows, :] + _dot(y_ref[rows, :], w_ref[...])
        o_ref[rows, :] = _layer_norm_rows(y, g, b)


def _outproj_ln(x, y, w_out, ln_g, ln_b, *, alpha, tm=1024):
    t, d = x.shape
    assert t % tm == 0
    vec = pl.BlockSpec((1, d), lambda i: (0, 0))
    return pl.pallas_call(
        functools.partial(_outproj_ln_kernel, alpha=alpha),
        grid=(t // tm,),
        in_specs=[
            pl.BlockSpec((tm, d), lambda i: (i, 0)),
            pl.BlockSpec((tm, d), lambda i: (i, 0)),
            pl.BlockSpec((d, d), lambda i: (0, 0), pipeline_mode=pl.Buffered(1)),
            vec, vec,
        ],
        out_specs=pl.BlockSpec((tm, d), lambda i: (i, 0)),
        out_shape=jax.ShapeDtypeStruct((t, d), F32),
        compiler_params=_params("parallel"),
        name="outproj_ln",
    )(x, y, w_out, ln_g.reshape(1, d), ln_b.reshape(1, d))


def kernel(x, mem, ln1_g, ln1_b, ffn1_w_gu, ffn1_w_down, w_in, b_gate, q_norm_a, k_norm_a,
           na_rpb, w_mem_kv, w_oa, w_ob, w_om, w_out, ln2_g, ln2_b, ffn2_w_gu, ffn2_w_down,
           ln3_g, ln3_b):
    batch, seq, d = x.shape
    mem_len = mem.shape[1]
    depth = w_in.shape[0]
    alpha = (2 * depth) ** 0.25
    h = x.reshape(batch * seq, d)
    mem2d = mem.reshape(batch * mem_len, d)
    d_ff = ffn1_w_down.shape[1]
    n_tiles = batch * seq // FFN_TM
    for l in range(depth):
        ffn1 = functools.partial(_ffn_ln, ln_g=ln1_g[l], ln_b=ln1_b[l], alpha=alpha)
        head, w_gate1, w_up1, w_down1 = ffn1(
            h, ffn1_w_gu[l], ffn1_w_gu[l], d_ff, ffn1_w_down[l],
            n_tiles=1, tf=FFN_TF_F32, emit_weights=True)
        h = ffn1(h, w_gate1, w_up1, 0, w_down1, n_tiles=n_tiles, tf=FFN_TF, done=head)
        w_attn = w_in[l][:, :W_ATTN_COLS].astype(BF16)
        z, w_gate = _inproj(h, w_attn, w_in[l], q_norm_a[l], k_norm_a[l], seq)
        oa, w_gu2, w_down2, w_oa_b, w_ob_b, w_om_b, w_out_b = _gqa(
            z, batch, seq,
            cast_along=(ffn2_w_gu[l], ffn2_w_down[l], w_oa[l], w_ob[l], w_om[l], w_out[l]))
        ob, om = _local_attn(z, na_rpb[l], mem2d, w_mem_kv[l], batch, seq)
        y = _gated_merge(h, oa, ob, om, w_gate, b_gate[l], w_oa_b, w_ob_b, w_om_b)
        h = _outproj_ln(h, y, w_out_b, ln2_g[l], ln2_b[l], alpha=alpha)
        h = _ffn_ln(h, w_gu2, w_gu2, d_ff, w_down2, ln3_g[l], ln3_b[l], alpha=alpha,
                    n_tiles=n_tiles, tf=FFN_TF)
    return h.reshape(batch, seq, d)
```

```python
import functools

import numpy as np
import jax
import jax.numpy as jnp
from jax import lax
from jax.experimental import pallas as pl
from jax.experimental.pallas import tpu as pltpu

HEAD_DIM = 128
GQA_Q_HEADS = 8
GQA_KV_HEADS = 2
NA_HEADS = 4
MEM_HEADS = 4
GRID_W = 64
NA_ROWS_MAX = 8
NA_COLS = 16
ROPE_THETA = 10000.0
LN_EPS = 1e-5
RMS_EPS = 1e-6
N_BRANCHES = 3
MASK_VALUE = -1e30

WA_Q = GQA_Q_HEADS * HEAD_DIM
WA_KV = GQA_KV_HEADS * HEAD_DIM
WB = NA_HEADS * HEAD_DIM
WM = MEM_HEADS * HEAD_DIM
COL_KA = WA_Q
COL_VA = COL_KA + WA_KV
COL_QB = COL_VA + WA_KV
COL_KB = COL_QB + WB
COL_VB = COL_KB + WB
COL_QM = COL_VB + WB
W_ATTN_COLS = COL_QM + WM

V7X_VMEM_BYTES = 64 * 1024 * 1024
VMEM_LIMIT_BYTES = V7X_VMEM_BYTES - 8 * 1024 * 1024

BF16 = jnp.bfloat16
F32 = jnp.float32
BF16_ROW_TILE = 16


def _params(*semantics):
    return pltpu.CompilerParams(dimension_semantics=semantics,
                                vmem_limit_bytes=VMEM_LIMIT_BYTES)


def _dot(a, b):
    return jnp.dot(a, b, preferred_element_type=F32)


def _dot_nt(a, b):
    return lax.dot_general(a, b, (((1,), (1,)), ((), ())), preferred_element_type=F32)


def _layer_norm_rows(y, g, b):
    mu = jnp.mean(y, axis=-1, keepdims=True)
    yc = y - mu
    var = jnp.mean(yc * yc, axis=-1, keepdims=True)
    return yc * lax.rsqrt(var + LN_EPS) * g + b


LN_ROW_CHUNK = 256


LN_OVERLAP_ROWS = 256


def _for_row_chunks(n_rows, body):
    def step(r, carry):
        body(pl.ds(pl.multiple_of(r * LN_ROW_CHUNK, LN_ROW_CHUNK), LN_ROW_CHUNK))
        return carry
    lax.fori_loop(0, n_rows // LN_ROW_CHUNK, step, 0)


FFN_TM = 1024
FFN_TF = 512
FFN_TF_F32 = 256


def _when_all(*conds):
    traced = [c for c in conds if c is not True]
    if not traced:
        return lambda f: f()
    cond = traced[0]
    for c in traced[1:]:
        cond = jnp.logical_and(cond, c)
    return pl.when(cond)


def _ffn_ln_kernel(x_ref, wg_ref, wu_ref, wd_ref, g_ref, b_ref, *rest, alpha, n_done,
                   emit_weights):
    if n_done:
        done_ref, o_ref, xb_ref, done_sem = rest
    elif emit_weights:
        o_ref, wg_out, wu_out, wd_out, xb_ref = rest
    else:
        o_ref, xb_ref = rest
    i = pl.program_id(0)
    j = pl.program_id(1)
    tm = x_ref.shape[0]
    live = True
    if n_done:
        live = i >= n_done

        @pl.when(jnp.logical_and(i < n_done, j == 0))
        def _take_done_tile():
            rows = pl.ds(pl.multiple_of(i * tm, tm), tm)
            copy = pltpu.make_async_copy(done_ref.at[rows, :], o_ref, done_sem)
            copy.start()
            copy.wait()

    @_when_all(live, j == 0)
    def _init():
        def body(rows):
            x = x_ref[rows, :]
            xb_ref[rows, :] = x.astype(BF16)
            o_ref[rows, :] = alpha * x
        _for_row_chunks(tm, body)

    @_when_all(live)
    def _accumulate():
        wg = wg_ref[...].astype(BF16)
        wu = wu_ref[...].astype(BF16)
        wd = wd_ref[...].astype(BF16)
        if emit_weights:
            wg_out[...] = wg
            wu_out[...] = wu
            wd_out[...] = wd
        xb = xb_ref[...]
        gate = _dot(xb, wg)
        up = _dot(xb, wu)
        h = (gate * jax.nn.sigmoid(gate) * (0.5 * up)).astype(BF16)
        o_ref[...] += _dot(h, wd)

    @_when_all(live, j == pl.num_programs(1) - 1)
    def _finish():
        g = g_ref[...]
        b = b_ref[...]

        def body(rows):
            o_ref[rows, :] = _layer_norm_rows(o_ref[rows, :], g, b)
        _for_row_chunks(tm, body)


def _ffn_ln(x, w_gate, w_up, up_col0, w_down, ln_g, ln_b, *, alpha, n_tiles, tf, tm=FFN_TM,
            done=None, emit_weights=False):
    t, d = x.shape
    f = w_down.shape[0]
    assert n_tiles * tm <= t and f % tf == 0 and up_col0 % tf == 0
    n_done = 0 if done is None else done.shape[0] // tm
    assert not (n_done and emit_weights) and (done is None or done.shape == (n_done * tm, d))
    nf = f // tf
    up0 = up_col0 // tf

    def chunk(i, j):
        return jnp.where(i >= n_done, j, 0) if n_done else j

    row_spec = pl.BlockSpec((tm, d), lambda i, j: (i, 0))
    vec = pl.BlockSpec((1, d), lambda i, j: (0, 0))
    gate_spec = pl.BlockSpec((d, tf), lambda i, j: (0, chunk(i, j)))
    down_spec = pl.BlockSpec((tf, d), lambda i, j: (chunk(i, j), 0))
    x_index = lambda i, j: (jnp.maximum(i, n_done), 0)
    in_specs = [
        pl.BlockSpec((tm, d), x_index),
        gate_spec,
        pl.BlockSpec((d, tf), lambda i, j: (0, chunk(i, j) + up0)),
        down_spec, vec, vec,
    ]
    args = [x, w_gate, w_up, w_down, ln_g.reshape(1, d), ln_b.reshape(1, d)]
    n_piped = len(args)
    scratch = [pltpu.VMEM((tm, d), BF16)]
    if n_done:
        args.append(done)
        scratch.append(pltpu.SemaphoreType.DMA(()))
    out_specs = [row_spec]
    out_shape = [jax.ShapeDtypeStruct((n_tiles * tm, d), F32)]
    if emit_weights:
        out_specs += [gate_spec, gate_spec, down_spec]
        out_shape += [jax.ShapeDtypeStruct((d, f), BF16), jax.ShapeDtypeStruct((d, f), BF16),
                      jax.ShapeDtypeStruct((f, d), BF16)]
    n_out = len(out_shape)
    step = functools.partial(_ffn_ln_kernel, alpha=alpha, n_done=n_done,
                             emit_weights=emit_weights)

    def pipelined(*refs):
        hbm_in = refs[:n_piped]
        hbm_done = refs[n_piped:len(args)]
        hbm_out = refs[len(args):len(args) + n_out]
        scratches = refs[len(args) + n_out:]

        def body(*blocks):
            ins, outs = blocks[:n_piped], blocks[n_piped:n_piped + n_out]
            scr = blocks[n_piped + n_out:]
            if n_done:
                step(*ins, hbm_done[0], outs[0], scr[0], scr[1])
            else:
                step(*ins, *outs, scr[0])

        pltpu.emit_pipeline(body, grid=(n_tiles, nf), in_specs=in_specs,
                            out_specs=out_specs)(*hbm_in, *hbm_out, scratches=scratches)

    hbm = pl.BlockSpec(memory_space=pl.ANY)
    res = pl.pallas_call(
        pipelined,
        in_specs=[hbm] * len(args),
        out_specs=[hbm] * n_out,
        out_shape=out_shape,
        scratch_shapes=scratch,
        compiler_params=pltpu.CompilerParams(vmem_limit_bytes=VMEM_LIMIT_BYTES),
        name="ffn_ln",
    )(*args)
    return res if emit_weights else res[0]


def _rope_tables(seq):
    t = np.arange(seq)
    pos = np.stack([t // GRID_W, t % GRID_W], axis=1).astype(np.float32)
    lane = np.arange(HEAD_DIM)
    half = lane // (HEAD_DIM // 2)
    within = lane % (HEAD_DIM // 2)
    quarter = HEAD_DIM // 4
    first = within < quarter
    freq = within % quarter
    axis_rot = HEAD_DIM // 2
    inv = jnp.asarray(ROPE_THETA, F32) ** (
        -jnp.arange(0, axis_rot, 2, dtype=F32) / axis_rot)
    ang = jnp.asarray(pos)[:, half] * inv[freq][None, :]
    cos = jnp.cos(ang)
    sin = jnp.sin(ang)
    first = jnp.asarray(first)[None, :]
    sin_up = jnp.where(first, -sin, 0.0)
    sin_dn = jnp.where(first, 0.0, sin)
    return cos, sin_up, sin_dn


def _inproj_kernel(x_ref, w_ref, qg_ref, kg_ref, cos_ref, sup_ref, sdn_ref, wfull_ref,
                   o_ref, wgate_ref):
    wgate_ref[...] = wfull_ref[:, W_ATTN_COLS:].astype(BF16)
    xb = x_ref[...].astype(BF16)
    cos = cos_ref[...]
    sup = sup_ref[...]
    sdn = sdn_ref[...]
    scale = HEAD_DIM ** -0.5
    quarter = HEAD_DIM // 4
    pair = 2 * HEAD_DIM

    def norm_rope(z, gain):
        z = z * lax.rsqrt(jnp.mean(z * z, axis=-1, keepdims=True) + RMS_EPS) * gain
        return (z * cos
                + pltpu.roll(z, HEAD_DIM - quarter, 1) * sup
                + pltpu.roll(z, quarter, 1) * sdn)

    for c in range(0, W_ATTN_COLS, pair):
        z2 = _dot(xb, w_ref[:, c:c + pair])
        for hh in range(2):
            col = c + hh * HEAD_DIM
            z = z2[:, hh * HEAD_DIM:(hh + 1) * HEAD_DIM]
            if col < COL_KA:
                z = norm_rope(z, qg_ref[...]) * scale
            elif col < COL_VA:
                z = norm_rope(z, kg_ref[...])
            elif COL_QB <= col < COL_KB or col >= COL_QM:
                z = z * scale
            o_ref[:, col:col + HEAD_DIM] = z.astype(BF16)


def _inproj(x, w_attn, w_full, q_gain, k_gain, seq, *, tm=512):
    t, d = x.shape
    assert t % tm == 0 and seq % tm == 0
    cos, sup, sdn = _rope_tables(seq)
    nseq = seq // tm
    n_steps = t // tm
    slab = d // n_steps
    assert slab * n_steps == d and slab % BF16_ROW_TILE == 0
    n_gate = w_full.shape[1] - W_ATTN_COLS
    tab_spec = pl.BlockSpec((tm, HEAD_DIM), lambda i: (i % nseq, 0))
    gain_spec = pl.BlockSpec((1, HEAD_DIM), lambda i: (0, 0))
    return pl.pallas_call(
        _inproj_kernel,
        grid=(n_steps,),
        in_specs=[
            pl.BlockSpec((tm, d), lambda i: (i, 0)),
            pl.BlockSpec((d, W_ATTN_COLS), lambda i: (0, 0), pipeline_mode=pl.Buffered(1)),
            gain_spec, gain_spec, tab_spec, tab_spec, tab_spec,
            pl.BlockSpec((slab, w_full.shape[1]), lambda i: (i, 0)),
        ],
        out_specs=[pl.BlockSpec((tm, W_ATTN_COLS), lambda i: (i, 0)),
                   pl.BlockSpec((slab, n_gate), lambda i: (i, 0))],
        out_shape=[jax.ShapeDtypeStruct((t, W_ATTN_COLS), BF16),
                   jax.ShapeDtypeStruct((d, n_gate), BF16)],
        compiler_params=_params("arbitrary"),
        name="inproj",
    )(x, w_attn, q_gain.reshape(1, HEAD_DIM), k_gain.reshape(1, HEAD_DIM), cos, sup, sdn,
      w_full)


def _with_ones_column(v):
    lane = lax.broadcasted_iota(jnp.int32, v.shape, 1)
    ones_col = jnp.where(lane == 0, 1.0, 0.0).astype(v.dtype)
    return jnp.concatenate([v, ones_col], axis=1)


def _softmax_pv(s, v_ext):
    m = jnp.max(s, axis=-1, keepdims=True)
    p = jnp.exp(s - m).astype(BF16)
    ov = _dot(p, v_ext)
    return ov[:, :HEAD_DIM] / ov[:, HEAD_DIM:HEAD_DIM + 1]


def _softmax_pv_few_keys(s, v):
    m = jnp.max(s, axis=-1, keepdims=True)
    p = jnp.exp(s - m)
    l = jnp.sum(p, axis=-1, keepdims=True)
    return _dot(p.astype(BF16), v) / l


GQA_GROUP = GQA_Q_HEADS // GQA_KV_HEADS
GQA_SUB_ROWS = 256


def _gqa_kernel(q_ref, k_ref, v_ref, *refs):
    n_cast = (len(refs) - 1) // 2
    o_ref = refs[n_cast]
    v_ext = _with_ones_column(v_ref[...])
    for r in range(0, q_ref.shape[0], GQA_SUB_ROWS):
        rows = slice(r, r + GQA_SUB_ROWS)
        for h in range(GQA_GROUP):
            cols = slice(h * HEAD_DIM, (h + 1) * HEAD_DIM)
            s = _dot_nt(q_ref[rows, cols], k_ref[...])
            o_ref[rows, cols] = _softmax_pv(s, v_ext).astype(BF16)
    for src_ref, dst_ref in zip(refs[:n_cast], refs[n_cast + 1:]):
        dst_ref[...] = src_ref[...].astype(BF16)


def _gqa(z, batch, seq, cast_along=(), *, tq=1024):
    t = z.shape[0]
    nq = seq // tq
    gw = GQA_GROUP * HEAD_DIM
    kcol = COL_KA // HEAD_DIM
    vcol = COL_VA // HEAD_DIM
    n_steps = batch * GQA_KV_HEADS * nq

    def slab(arr):
        rows = arr.shape[0] // n_steps
        assert rows * n_steps == arr.shape[0] and rows % BF16_ROW_TILE == 0
        return pl.BlockSpec((rows, arr.shape[1]),
                            lambda b, g, i: ((b * GQA_KV_HEADS + g) * nq + i, 0))

    slabs = [slab(a) for a in cast_along]
    in_specs = [
        pl.BlockSpec((tq, gw), lambda b, g, i: (b * nq + i, g)),
        pl.BlockSpec((seq, HEAD_DIM), lambda b, g, i: (b, kcol + g)),
        pl.BlockSpec((seq, HEAD_DIM), lambda b, g, i: (b, vcol + g)),
    ] + slabs
    out_specs = [pl.BlockSpec((tq, gw), lambda b, g, i: (b * nq + i, g))] + slabs
    n_in = len(in_specs)

    def pipelined(*refs):
        pltpu.emit_pipeline(_gqa_kernel, grid=(batch, GQA_KV_HEADS, nq), in_specs=in_specs,
                            out_specs=out_specs)(*refs[:n_in], *refs[n_in:])

    hbm = pl.BlockSpec(memory_space=pl.ANY)
    return pl.pallas_call(
        pipelined,
        in_specs=[hbm] * n_in,
        out_specs=[hbm] * len(out_specs),
        out_shape=[jax.ShapeDtypeStruct((t, WA_Q), BF16)]
        + [jax.ShapeDtypeStruct(a.shape, BF16) for a in cast_along],
        compiler_params=pltpu.CompilerParams(vmem_limit_bytes=VMEM_LIMIT_BYTES),
        name="gqa",
    )(z, z, z, *cast_along)


NA_Q_ROWS = 4
NA_K_ROWS = NA_Q_ROWS + NA_ROWS_MAX
NA_TQ = NA_Q_ROWS * GRID_W
NA_TK = NA_K_ROWS * GRID_W


def _na_key_row_start(qblock, rows):
    return np.clip(qblock * NA_Q_ROWS - NA_ROWS_MAX // 2, 0, rows - NA_K_ROWS)


def _na_bias_tables(rpb, rows):
    heads, n_dr, n_dc = rpb.shape
    nblocks = rows // NA_Q_ROWS
    kr_win = min(NA_ROWS_MAX, rows)
    period = 2 * GRID_W
    assert period >= GRID_W + NA_COLS and n_dc == 2 * NA_COLS - 1
    v = rpb.astype(F32)
    vpad = jnp.concatenate(
        [v[..., NA_COLS - 1:], jnp.zeros((heads, n_dr, period - n_dc), F32), v[..., :NA_COLS - 1]],
        axis=-1)
    skew = jnp.tile(vpad, (1, 1, GRID_W))[..., :GRID_W * (period - 1)]
    toep = skew.reshape(heads, n_dr, GRID_W, period - 1)[..., :GRID_W]
    qc = np.arange(GRID_W)
    cs = np.clip(qc - NA_COLS // 2, 0, GRID_W - NA_COLS)
    ok_c = (qc[None, :] >= cs[:, None]) & (qc[None, :] < cs[:, None] + NA_COLS)
    toep = jnp.where(jnp.asarray(ok_c)[None, None], toep, MASK_VALUE)
    masked = jnp.full((heads, GRID_W, GRID_W), MASK_VALUE, F32)
    tabs = []
    for qblock in (0, 1, nblocks - 1):
        ks = _na_key_row_start(qblock, rows)
        q_rows = []
        for qi in range(NA_Q_ROWS):
            qr = qblock * NA_Q_ROWS + qi
            rs = np.clip(qr - kr_win // 2, 0, rows - kr_win)
            blocks = []
            for kj in range(NA_K_ROWS):
                kr = ks + kj
                inside = rs <= kr < rs + kr_win
                blocks.append(toep[:, kr - qr + NA_ROWS_MAX - 1] if inside else masked)
            q_rows.append(jnp.concatenate(blocks, axis=-1))
        tabs.append(jnp.concatenate(q_rows, axis=-2))
    return jnp.stack(tabs)


MEM_Q_ROWS = 512


def _local_attn_kernel(qb_ref, kb_ref, vb_ref, bias_ref, qm_ref, mem_ref, wkv_ref,
                       ob_ref, om_ref):
    rows = qb_ref.shape[0] // GRID_W
    n_blocks = rows // NA_Q_ROWS
    for i in range(n_blocks):
        pattern = 0 if i == 0 else (2 if i == n_blocks - 1 else 1)
        qrows = slice(i * NA_TQ, (i + 1) * NA_TQ)
        start = int(_na_key_row_start(i, rows)) * GRID_W
        krows = slice(start, start + NA_TK)
        for h in range(NA_HEADS):
            cols = slice(h * HEAD_DIM, (h + 1) * HEAD_DIM)
            s = _dot_nt(qb_ref[qrows, cols], kb_ref[krows, cols]) + bias_ref[pattern, h]
            v_ext = _with_ones_column(vb_ref[krows, cols])
            ob_ref[qrows, cols] = _softmax_pv(s, v_ext).astype(BF16)
    kvm = _dot(mem_ref[...].astype(BF16), wkv_ref[...].astype(BF16)).astype(BF16)
    for r in range(0, qm_ref.shape[0], MEM_Q_ROWS):
        qrows = slice(r, r + MEM_Q_ROWS)
        for h in range(MEM_HEADS):
            cols = slice(h * HEAD_DIM, (h + 1) * HEAD_DIM)
            s = _dot_nt(qm_ref[qrows, cols], kvm[:, cols])
            v = kvm[:, WM + h * HEAD_DIM:WM + (h + 1) * HEAD_DIM]
            om_ref[qrows, cols] = _softmax_pv_few_keys(s, v).astype(BF16)


def _local_attn(z, rpb, mem2d, w_mem_kv, batch, seq):
    t = z.shape[0]
    mem_len = mem2d.shape[0] // batch
    rows = seq // GRID_W
    assert rows % NA_Q_ROWS == 0 and rows >= NA_K_ROWS and seq % MEM_Q_ROWS == 0
    bias = _na_bias_tables(rpb, rows)

    def per_batch(col_block):
        return pl.BlockSpec((seq, WB), lambda b: (b, col_block))

    def resident(arr):
        return pl.BlockSpec(arr.shape, lambda b: (0,) * arr.ndim, pipeline_mode=pl.Buffered(1))

    return pl.pallas_call(
        _local_attn_kernel,
        grid=(batch,),
        in_specs=[
            per_batch(COL_QB // WB), per_batch(COL_KB // WB), per_batch(COL_VB // WB),
            resident(bias),
            pl.BlockSpec((seq, WM), lambda b: (b, COL_QM // WM)),
            pl.BlockSpec((mem_len, mem2d.shape[1]), lambda b: (b, 0)),
            resident(w_mem_kv),
        ],
        out_specs=[pl.BlockSpec((seq, WB), lambda b: (b, 0)),
                   pl.BlockSpec((seq, WM), lambda b: (b, 0))],
        out_shape=[jax.ShapeDtypeStruct((t, WB), BF16),
                   jax.ShapeDtypeStruct((t, WM), BF16)],
        compiler_params=_params("parallel"),
        name="local_attn",
    )(z, z, z, bias, z, mem2d, w_mem_kv)


MXU_COLS = 256


def _gated_merge_kernel(x_ref, oa_ref, ob_ref, om_ref, wg_ref, bg_ref,
                        woa_ref, wob_ref, wom_ref, y_ref):
    d = x_ref.shape[1]
    xb = x_ref[...].astype(BF16)
    branches = ((oa_ref[...], woa_ref), (ob_ref[...], wob_ref), (om_ref[...], wom_ref))
    for c in range(0, d, MXU_COLS):
        cols = slice(c, c + MXU_COLS)
        y = None
        for k, (o, wo_ref) in enumerate(branches):
            gcols = slice(k * d + c, k * d + c + MXU_COLS)
            gate = jax.nn.sigmoid(_dot(xb, wg_ref[:, gcols]) + bg_ref[:, gcols])
            term = gate * _dot(o, wo_ref[:, cols])
            y = term if y is None else y + term
        y_ref[:, cols] = y.astype(BF16)


def _gated_merge(x, oa, ob, om, w_gate, b_gate, w_oa, w_ob, w_om, *, tm=512):
    t, d = x.shape
    assert t % tm == 0 and d % MXU_COLS == 0 and w_gate.shape == (d, N_BRANCHES * d)

    def rows(width):
        return pl.BlockSpec((tm, width), lambda i: (i, 0))

    def resident(arr):
        return pl.BlockSpec(arr.shape, lambda i: (0, 0), pipeline_mode=pl.Buffered(1))

    b_gate = b_gate.reshape(1, N_BRANCHES * d)
    return pl.pallas_call(
        _gated_merge_kernel,
        grid=(t // tm,),
        in_specs=[rows(d), rows(WA_Q), rows(WB), rows(WM),
                  resident(w_gate), resident(b_gate),
                  resident(w_oa), resident(w_ob), resident(w_om)],
        out_specs=rows(d),
        out_shape=jax.ShapeDtypeStruct((t, d), BF16),
        compiler_params=_params("parallel"),
        name="gated_merge",
    )(x, oa, ob, om, w_gate, b_gate, w_oa, w_ob, w_om)


def _outproj_ln_kernel(x_ref, y_ref, w_ref, g_ref, b_ref, o_ref, *, alpha):
    tm = x_ref.shape[0]
    g = g_ref[...]
    b = b_ref[...]
    for r in range(0, tm, LN_OVERLAP_ROWS):
        rows = slice(r, r + LN_OVERLAP_ROWS)
        y = alpha * x_ref[rows, :] + _dot(y_ref[rows, :], w_ref[...])
        o_ref[rows, :] = _layer_norm_rows(y, g, b)


def _outproj_ln(x, y, w_out, ln_g, ln_b, *, alpha, tm=1024):
    t, d = x.shape
    assert t % tm == 0
    vec = pl.BlockSpec((1, d), lambda i: (0, 0))
    return pl.pallas_call(
        functools.partial(_outproj_ln_kernel, alpha=alpha),
        grid=(t // tm,),
        in_specs=[
            pl.BlockSpec((tm, d), lambda i: (i, 0)),
            pl.BlockSpec((tm, d), lambda i: (i, 0)),
            pl.BlockSpec((d, d), lambda i: (0, 0), pipeline_mode=pl.Buffered(1)),
            vec, vec,
        ],
        out_specs=pl.BlockSpec((tm, d), lambda i: (i, 0)),
        out_shape=jax.ShapeDtypeStruct((t, d), F32),
        compiler_params=_params("parallel"),
        name="outproj_ln",
    )(x, y, w_out, ln_g.reshape(1, d), ln_b.reshape(1, d))


def kernel(x, mem, ln1_g, ln1_b, ffn1_w_gu, ffn1_w_down, w_in, b_gate, q_norm_a, k_norm_a,
           na_rpb, w_mem_kv, w_oa, w_ob, w_om, w_out, ln2_g, ln2_b, ffn2_w_gu, ffn2_w_down,
           ln3_g, ln3_b):
    batch, seq, d = x.shape
    mem_len = mem.shape[1]
    depth = w_in.shape[0]
    alpha = (2 * depth) ** 0.25
    h = x.reshape(batch * seq, d)
    mem2d = mem.reshape(batch * mem_len, d)
    d_ff = ffn1_w_down.shape[1]
    n_tiles = batch * seq // FFN_TM
    for l in range(depth):
        ffn1 = functools.partial(_ffn_ln, ln_g=ln1_g[l], ln_b=ln1_b[l], alpha=alpha)
        head, w_gate1, w_up1, w_down1 = ffn1(
            h, ffn1_w_gu[l], ffn1_w_gu[l], d_ff, ffn1_w_down[l],
            n_tiles=1, tf=FFN_TF_F32, emit_weights=True)
        h = ffn1(h, w_gate1, w_up1, 0, w_down1, n_tiles=n_tiles, tf=FFN_TF, done=head)
        w_attn = w_in[l][:, :W_ATTN_COLS].astype(BF16)
        z, w_gate = _inproj(h, w_attn, w_in[l], q_norm_a[l], k_norm_a[l], seq)
        oa, w_gu2, w_down2, w_oa_b, w_ob_b, w_om_b, w_out_b = _gqa(
            z, batch, seq,
            cast_along=(ffn2_w_gu[l], ffn2_w_down[l], w_oa[l], w_ob[l], w_om[l], w_out[l]))
        ob, om = _local_attn(z, na_rpb[l], mem2d, w_mem_kv[l], batch, seq)
        y = _gated_merge(h, oa, ob, om, w_gate, b_gate[l], w_oa_b, w_ob_b, w_om_b)
        h = _outproj_ln(h, y, w_out_b, ln2_g[l], ln2_b[l], alpha=alpha)
        h = _ffn_ln(h, w_gu2, w_gu2, d_ff, w_down2, ln3_g[l], ln3_b[l], alpha=alpha,
                    n_tiles=n_tiles, tf=FFN_TF)
    return h.reshape(batch, seq, d)
```
